```python
import jax, jax.numpy as jnp
from jax import lax
import numpy as np

D_MODEL = 1024
BATCH = 8
SEQ = 2048
DEPTH = 4
DEC_BATCH = 32
DEC_SEQ = 1
PAST_LEN = 8192
PAGE_SIZE = 128

N_HEADS_A = 8
HEAD_DIM = 64
A_WIDTH = N_HEADS_A * HEAD_DIM
DILATED_BRANCHES = ((128, 1), (512, 4), (2048, 16))
MAX_WINDOW = 2048
ROPE_THETA = 10000.0
Q_BLOCK = 128
B_WIDTH = D_MODEL // 2
N_GROUPS_B = 8
B_GROUP_DIM = B_WIDTH // N_GROUPS_B
CHUNK = 128
D_CONV = D_MODEL
CONV_WIDTH = 31
D_FF = 4 * D_MODEL
EPS = 1e-6
N_AB_LAYERS = (DEPTH + 1) // 2
N_C_LAYERS = DEPTH // 2
IN_AB_WIDTH = 3 * A_WIDTH + 2 * B_WIDTH

kernel_name = "hybrid_dilated_gmlp_conformer_step"


def rms_norm(x, g):
    xf = x.astype(jnp.float32)
    y = xf * lax.rsqrt(jnp.mean(xf * xf, axis=-1, keepdims=True) + EPS)
    return (y * g.astype(jnp.float32)).astype(x.dtype)


def layer_norm(x, g, b):
    xf = x.astype(jnp.float32)
    mu = jnp.mean(xf, axis=-1, keepdims=True)
    xc = xf - mu
    y = xc * lax.rsqrt(jnp.mean(xc * xc, axis=-1, keepdims=True) + EPS)
    return (y * g.astype(jnp.float32) + b.astype(jnp.float32)).astype(x.dtype)


def rope(x, pos):
    inv = ROPE_THETA ** (-jnp.arange(0, HEAD_DIM, 2, dtype=jnp.float32) / HEAD_DIM)
    ang = pos.astype(jnp.float32)[:, None] * inv[None, :]
    cos = jnp.cos(ang)[None, :, None, :]
    sin = jnp.sin(ang)[None, :, None, :]
    xf = x.astype(jnp.float32)
    x1, x2 = xf[..., :HEAD_DIM // 2], xf[..., HEAD_DIM // 2:]
    out = jnp.concatenate([x1 * cos - x2 * sin, x2 * cos + x1 * sin], axis=-1)
    return out.astype(x.dtype)


def masked_softmax(s, valid):
    s = jnp.where(valid, s, -jnp.inf)
    m = jnp.max(s, axis=-1, keepdims=True)
    p = jnp.exp(s - m)
    den = jnp.sum(p, axis=-1, keepdims=True)
    return p / den, (m + jnp.log(den))[..., 0]


def ab_project(h, w_in, q_g, k_g, vb_g, vb_b, pos):
    z = h @ w_in
    n, l, _ = z.shape
    q, k, v, u_b, v_b = jnp.split(z, [A_WIDTH, 2 * A_WIDTH, 3 * A_WIDTH, 3 * A_WIDTH + B_WIDTH], axis=-1)
    q = rope(rms_norm(q.reshape(n, l, N_HEADS_A, HEAD_DIM), q_g), pos)
    k = rope(rms_norm(k.reshape(n, l, N_HEADS_A, HEAD_DIM), k_g), pos)
    v = v.reshape(n, l, N_HEADS_A, HEAD_DIM)
    u_b = jax.nn.gelu(u_b)
    v_b = layer_norm(jax.nn.gelu(v_b), vb_g, vb_b)
    return q, k, v, u_b, v_b


def banded_attention(q, k, v, n_back):
    n, l, h, dh = q.shape
    nb = -(-l // Q_BLOCK)
    lp = nb * Q_BLOCK
    pad = ((0, 0), (0, lp - l), (0, 0), (0, 0))
    q, k, v = jnp.pad(q, pad), jnp.pad(k, pad), jnp.pad(v, pad)
    qb = q.reshape(n, nb, Q_BLOCK, h, dh)

    def two_blocks(t):
        cur = t.reshape(n, nb, Q_BLOCK, h, dh)
        prev = jnp.pad(t, ((0, 0), (Q_BLOCK, 0), (0, 0), (0, 0)))[:, :lp].reshape(n, nb, Q_BLOCK, h, dh)
        return jnp.concatenate([prev, cur], axis=2)

    kw, vw = two_blocks(k), two_blocks(v)
    s = jnp.einsum('nbqhd,nbkhd->nbhqk', qb, kw).astype(jnp.float32) * (dh ** -0.5)
    blk = jnp.arange(nb)[:, None, None] * Q_BLOCK
    qpos = blk + jnp.arange(Q_BLOCK)[None, :, None]
    kpos = blk - Q_BLOCK + jnp.arange(2 * Q_BLOCK)[None, None, :]
    dist = qpos - kpos
    valid = (dist >= 0) & (dist <= n_back) & (kpos >= 0)
    p, lse = masked_softmax(s, valid[None, :, None])
    o = jnp.einsum('nbhqk,nbkhd->nbqhd', p.astype(v.dtype), vw).reshape(n, lp, h, dh)[:, :l]
    lse = lse.transpose(0, 1, 3, 2).reshape(n, lp, h)[:, :l]
    return o, lse


def combine_branches(outs, lses):
    w = jax.nn.softmax(jnp.stack(lses, axis=0), axis=0)
    o = jnp.sum(w[..., None] * jnp.stack(outs, axis=0).astype(jnp.float32), axis=0)
    return o.astype(outs[0].dtype)


def dilated_attention_prompt(q, k, v):
    b, t, h, dh = q.shape
    outs, lses = [], []
    for window, dil in DILATED_BRANCHES:
        l = t // dil

        def to_sub(x):
            return x.reshape(b, l, dil, h, dh).transpose(0, 2, 1, 3, 4).reshape(b * dil, l, h, dh)

        o, lse = banded_attention(to_sub(q), to_sub(k), to_sub(v), window // dil)
        outs.append(o.reshape(b, dil, l, h, dh).transpose(0, 2, 1, 3, 4).reshape(b, t, h, dh))
        lses.append(lse.reshape(b, dil, l, h).transpose(0, 2, 1, 3).reshape(b, t, h))
    return combine_branches(outs, lses)


def dilated_attention_sample(q, k_all, v_all, q_pos, buf_start):
    outs, lses = [], []
    for window, dil in DILATED_BRANCHES:
        n_back = window // dil
        key_pos = q_pos[:, None] - dil * jnp.arange(n_back + 1)[None, :]
        valid = key_pos >= 0
        idx = jnp.clip(key_pos - buf_start, 0, k_all.shape[1] - 1)
        kg = jnp.take(k_all, idx, axis=1)
        vg = jnp.take(v_all, idx, axis=1)
        s = jnp.einsum('nshd,nskhd->nhsk', q, kg).astype(jnp.float32) * (HEAD_DIM ** -0.5)
        p, lse = masked_softmax(s, valid[None, None])
        outs.append(jnp.einsum('nhsk,nskhd->nshd', p.astype(vg.dtype), vg))
        lses.append(lse.transpose(0, 2, 1))
    return combine_branches(outs, lses)


def spatial_gate(u, v, w_s, b_s):
    n, l, _ = v.shape
    nc = -(-l // CHUNK)
    lp = nc * CHUNK
    vp = jnp.pad(v, ((0, 0), (0, lp - l), (0, 0))).reshape(n, nc, CHUNK, N_GROUPS_B, B_GROUP_DIM)
    mask = jnp.tril(jnp.ones((CHUNK, CHUNK), dtype=bool))
    w = jnp.where(mask[None], w_s, jnp.zeros_like(w_s))
    mixed = jnp.einsum('gij,ncjgd->ncigd', w, vp) + b_s.T[None, None, :, :, None]
    mixed = mixed.reshape(n, lp, B_WIDTH)[:, :l]
    return u * mixed


def conv_module(h, buf, w_in, w_dw, b_dw, g, b, w_out):
    z = h @ w_in
    a, gate = jnp.split(z, 2, axis=-1)
    x = a * jax.nn.sigmoid(gate)
    xc = jnp.concatenate([buf.astype(x.dtype), x], axis=1)
    y = lax.conv_general_dilated(xc, w_dw[:, None, :], window_strides=(1,), padding='VALID',
                                 dimension_numbers=('NWC', 'WIO', 'NWC'), feature_group_count=D_CONV) + b_dw
    y = jax.nn.silu(layer_norm(y, g, b))
    return y @ w_out, xc[:, -(CONV_WIDTH - 1):]


def ffn(h, w_up, w_down):
    return jnp.square(jax.nn.relu(h @ w_up)) @ w_down


def setup_inputs(seed: int = 0) -> dict:
    key = jax.random.key(seed)
    ks = jax.random.split(key, 24)
    win_buf = min(MAX_WINDOW, PAST_LEN)

    def nrm(k, shape, scale):
        return jax.random.normal(k, shape, jnp.float32) * scale

    def gain(k, shape):
        return 1.0 + 0.05 * jax.random.normal(k, shape, jnp.float32)

    return {
        "x_prompt": nrm(ks[0], (BATCH, SEQ, D_MODEL), 1.0),
        "x_sample": nrm(ks[1], (DEC_BATCH, DEC_SEQ, D_MODEL), 1.0),
        "cache_a_k": nrm(ks[2], (N_AB_LAYERS, DEC_BATCH, win_buf, N_HEADS_A, HEAD_DIM), 1.0),
        "cache_a_v": nrm(ks[3], (N_AB_LAYERS, DEC_BATCH, win_buf, N_HEADS_A, HEAD_DIM), 1.0),
        "state_c_conv": nrm(ks[4], (N_C_LAYERS, DEC_BATCH, CONV_WIDTH - 1, D_CONV), 0.5),
        "norm_mix_g": gain(ks[5], (DEPTH, D_MODEL)),
        "norm_ffn_g": gain(ks[6], (DEPTH, D_MODEL)),
        "w_ffn_up": nrm(ks[7], (DEPTH, D_MODEL, D_FF), D_MODEL ** -0.5),
        "w_ffn_down": nrm(ks[8], (DEPTH, D_FF, D_MODEL), D_FF ** -0.5),
        "w_in_ab": nrm(ks[9], (N_AB_LAYERS, D_MODEL, IN_AB_WIDTH), D_MODEL ** -0.5),
        "q_norm_g": gain(ks[10], (N_AB_LAYERS, HEAD_DIM)),
        "k_norm_g": gain(ks[11], (N_AB_LAYERS, HEAD_DIM)),
        "vb_norm_g": gain(ks[12], (N_AB_LAYERS, B_WIDTH)),
        "vb_norm_b": nrm(ks[13], (N_AB_LAYERS, B_WIDTH), 0.02),
        "w_spatial": nrm(ks[14], (N_AB_LAYERS, N_GROUPS_B, CHUNK, CHUNK), CHUNK ** -0.5),
        "b_spatial": gain(ks[15], (N_AB_LAYERS, N_GROUPS_B, CHUNK)),
        "w_out_ab": nrm(ks[16], (N_AB_LAYERS, A_WIDTH + B_WIDTH, D_MODEL), (A_WIDTH + B_WIDTH) ** -0.5),
        "w_c_in": nrm(ks[17], (N_C_LAYERS, D_MODEL, 2 * D_CONV), D_MODEL ** -0.5),
        "w_c_dw": nrm(ks[18], (N_C_LAYERS, CONV_WIDTH, D_CONV), CONV_WIDTH ** -0.5),
        "b_c_dw": nrm(ks[19], (N_C_LAYERS, D_CONV), 0.02),
        "c_norm_g": gain(ks[20], (N_C_LAYERS, D_CONV)),
        "c_norm_b": nrm(ks[21], (N_C_LAYERS, D_CONV), 0.02),
        "w_c_out": nrm(ks[22], (N_C_LAYERS, D_CONV, D_MODEL), D_CONV ** -0.5),
    }


def reference(x_prompt, x_sample, cache_a_k, cache_a_v, state_c_conv, norm_mix_g, norm_ffn_g, w_ffn_up, w_ffn_down,
              w_in_ab, q_norm_g, k_norm_g, vb_norm_g, vb_norm_b, w_spatial, b_spatial, w_out_ab,
              w_c_in, w_c_dw, b_c_dw, c_norm_g, c_norm_b, w_c_out):
    pos_p = jnp.arange(SEQ, dtype=jnp.int32)
    pos_s = PAST_LEN + jnp.arange(DEC_SEQ, dtype=jnp.int32)
    win_buf = cache_a_k.shape[2]
    buf_start = PAST_LEN - win_buf
    prompt_buf = min(MAX_WINDOW, SEQ)
    last_chunk_start = ((SEQ - 1) // CHUNK) * CHUNK

    hp, hs = x_prompt, x_sample
    ak_p, av_p, ak_s, av_s, bv_p, bv_s, cc_p, cc_s = [], [], [], [], [], [], [], []
    for layer in range(DEPTH):
        j = layer // 2
        n_p = rms_norm(hp, norm_mix_g[layer])
        n_s = rms_norm(hs, norm_mix_g[layer])
        if layer % 2 == 0:
            qp, kp, vp, up, vbp = ab_project(n_p, w_in_ab[j], q_norm_g[j], k_norm_g[j], vb_norm_g[j], vb_norm_b[j], pos_p)
            qs, kss, vss, us, vbs = ab_project(n_s, w_in_ab[j], q_norm_g[j], k_norm_g[j], vb_norm_g[j], vb_norm_b[j], pos_s)
            att_p = dilated_attention_prompt(qp, kp, vp).reshape(BATCH, SEQ, A_WIDTH)
            k_all = jnp.concatenate([cache_a_k[j].astype(kss.dtype), kss], axis=1)
            v_all = jnp.concatenate([cache_a_v[j].astype(vss.dtype), vss], axis=1)
            att_s = dilated_attention_sample(qs, k_all, v_all, pos_s, buf_start).reshape(DEC_BATCH, DEC_SEQ, A_WIDTH)
            gate_p = spatial_gate(up, vbp, w_spatial[j], b_spatial[j])
            gate_s = spatial_gate(us, vbs, w_spatial[j], b_spatial[j])
            mix_p = jnp.concatenate([att_p, gate_p], axis=-1) @ w_out_ab[j]
            mix_s = jnp.concatenate([att_s, gate_s], axis=-1) @ w_out_ab[j]
            ak_p.append(kp[:, SEQ - prompt_buf:])
            av_p.append(vp[:, SEQ - prompt_buf:])
            ak_s.append(kss)
            av_s.append(vss)
            bv_p.append(vbp[:, last_chunk_start:])
            bv_s.append(vbs)
        else:
            zero_buf = jnp.zeros((BATCH, CONV_WIDTH - 1, D_CONV), dtype=n_p.dtype)
            mix_p, new_cp = conv_module(n_p, zero_buf, w_c_in[j], w_c_dw[j], b_c_dw[j], c_norm_g[j], c_norm_b[j], w_c_out[j])
            mix_s, new_cs = conv_module(n_s, state_c_conv[j], w_c_in[j], w_c_dw[j], b_c_dw[j], c_norm_g[j], c_norm_b[j], w_c_out[j])
            cc_p.append(new_cp)
            cc_s.append(new_cs)
        hp = hp + mix_p
        hs = hs + mix_s
        hp = hp + ffn(rms_norm(hp, norm_ffn_g[layer]), w_ffn_up[layer], w_ffn_down[layer])
        hs = hs + ffn(rms_norm(hs, norm_ffn_g[layer]), w_ffn_up[layer], w_ffn_down[layer])

    return (hp, hs, jnp.stack(ak_p), jnp.stack(av_p), jnp.stack(ak_s), jnp.stack(av_s),
            jnp.stack(bv_p), jnp.stack(bv_s), jnp.stack(cc_p), jnp.stack(cc_s))
```

```python
import functools

import jax
import jax.numpy as jnp
from jax import lax
from jax.experimental import pallas as pl
from jax.experimental.pallas import tpu as pltpu

F32 = jnp.float32
BF16 = jnp.bfloat16

D_MODEL = 1024
N_HEADS = 8
HEAD_DIM = 64
A_WIDTH = N_HEADS * HEAD_DIM
B_WIDTH = 512
N_GROUPS = 8
CHUNK = 128
Q_BLOCK = 128
BRANCHES = ((128, 1), (512, 4), (2048, 16))
ROPE_THETA = 10000.0
CONV_WIDTH = 31
PAST_LEN = 8192
D_FF = 4 * D_MODEL
EPS = 1e-6
IN_AB = 3 * A_WIDTH + 2 * B_WIDTH

LANES = 128
HALO = 32
FF_CHUNK = 1024
NEG = -1e30
VMEM_LIMIT = 56 * 1024 * 1024


def _rmsnorm(x, g):
    ms = jnp.mean(x * x, axis=-1, keepdims=True)
    return x * lax.rsqrt(ms + EPS) * g


def _layernorm(x, g, b):
    mu = jnp.mean(x, axis=-1, keepdims=True)
    xc = x - mu
    var = jnp.mean(xc * xc, axis=-1, keepdims=True)
    return xc * lax.rsqrt(var + EPS) * g + b


def _dot(a, b):
    return jnp.dot(a, b, preferred_element_type=F32)


def _ffn(h1, g, wup_ref, wdn_ref):
    n = _rmsnorm(h1, g).astype(BF16)
    acc = h1
    for c in range(D_FF // FF_CHUNK):
        a = _dot(n, wup_ref[:, c * FF_CHUNK:(c + 1) * FF_CHUNK])
        a = jnp.maximum(a, 0.0)
        a = (a * a).astype(BF16)
        acc = acc + _dot(a, wdn_ref[c * FF_CHUNK:(c + 1) * FF_CHUNK, :])
    return acc


def _ab_in_kernel(h_ref, g_ref, w_ref, cos_ref, sin_ref, qg_ref, kg_ref, hm_ref, vbg_ref, vbb_ref,
                  q_ref, k_ref, v_ref, u_ref, vb_ref):
    n = _rmsnorm(h_ref[...], g_ref[...]).astype(BF16)
    cos = cos_ref[...]
    sin = sin_ref[...]
    hm = hm_ref[...]
    lane = lax.broadcasted_iota(jnp.int32, (1, LANES), 1)
    first_half = (lane & (HEAD_DIM // 2)) == 0

    def head_norm_rope(z, gain_ref, out_ref, scale):
        for c in range(A_WIDTH // LANES):
            sl = slice(c * LANES, (c + 1) * LANES)
            zc = z[:, sl]
            sq = zc * zc
            hi = sq.astype(BF16)
            lo = (sq - hi.astype(F32)).astype(BF16)
            ms = _dot(hi, hm) + _dot(lo, hm)
            y = zc * lax.rsqrt(ms + EPS) * gain_ref[:, sl]
            partner = jnp.where(first_half, pltpu.roll(y, LANES - HEAD_DIM // 2, 1),
                                pltpu.roll(y, HEAD_DIM // 2, 1))
            out = y * cos + partner * sin
            if scale != 1.0:
                out = out * scale
            out_ref[:, sl] = out

    head_norm_rope(_dot(n, w_ref[:, 0:A_WIDTH]), qg_ref, q_ref, HEAD_DIM ** -0.5)
    head_norm_rope(_dot(n, w_ref[:, A_WIDTH:2 * A_WIDTH]), kg_ref, k_ref, 1.0)
    v_ref[...] = _dot(n, w_ref[:, 2 * A_WIDTH:3 * A_WIDTH])
    u_ref[...] = jax.nn.gelu(_dot(n, w_ref[:, 3 * A_WIDTH:3 * A_WIDTH + B_WIDTH]))
    vb = jax.nn.gelu(_dot(n, w_ref[:, 3 * A_WIDTH + B_WIDTH:IN_AB]))
    vb_ref[...] = _layernorm(vb, vbg_ref[...], vbb_ref[...])


def _const_spec(shape):
    nd = len(shape)
    return pl.BlockSpec(shape, lambda *_: (0,) * nd, pipeline_mode=pl.Buffered(1))


def _ab_in(h, g, w_bf, cos_t, sin_t, qg, kg, hm, vbg, vbb, tm):
    t = h.shape[0]
    n_pos_blocks = cos_t.shape[0] // tm
    row = lambda i: (i, 0)
    tab = lambda i: (i % n_pos_blocks, 0)
    out = jax.ShapeDtypeStruct((t, A_WIDTH), F32)
    return pl.pallas_call(
        _ab_in_kernel,
        grid=(t // tm,),
        in_specs=[
            pl.BlockSpec((tm, D_MODEL), row), _const_spec((1, D_MODEL)), _const_spec((D_MODEL, IN_AB)),
            pl.BlockSpec((tm, LANES), tab), pl.BlockSpec((tm, LANES), tab),
            _const_spec((1, A_WIDTH)), _const_spec((1, A_WIDTH)), _const_spec((LANES, LANES)),
            _const_spec((1, B_WIDTH)), _const_spec((1, B_WIDTH)),
        ],
        out_specs=[pl.BlockSpec((tm, A_WIDTH), row)] * 5,
        out_shape=[out] * 5,
        compiler_params=pltpu.CompilerParams(dimension_semantics=("arbitrary",), vmem_limit_bytes=VMEM_LIMIT),
        name="ab_in",
    )(h, g, w_bf, cos_t, sin_t, qg, kg, hm, vbg, vbb)


def _attn_prompt_kernel(q_ref, k_ref, v_ref, o_ref, ob_scr, lse_scr, *, seq):
    lane = lax.broadcasted_iota(jnp.int32, (1, LANES), 1)
    head_masks = (lane < HEAD_DIM, lane >= HEAD_DIM)
    ri = lax.broadcasted_iota(jnp.int32, (Q_BLOCK, Q_BLOCK), 0)
    ci = lax.broadcasted_iota(jnp.int32, (Q_BLOCK, Q_BLOCK), 1)
    tri_cur = ci <= ri
    tri_prev = ci >= ri

    def rows(start, dil):
        if dil == 1:
            return pl.ds(pl.multiple_of(start, Q_BLOCK), Q_BLOCK)
        return pl.ds(start, Q_BLOCK, stride=dil)

    def block(branch, dil, start, with_prev):
        cur = rows(start, dil)
        q = q_ref[0, cur, :]
        kc = k_ref[0, cur, :].astype(BF16)
        vc = v_ref[0, cur, :].astype(BF16)
        if with_prev:
            prev = rows(start - dil * Q_BLOCK, dil)
            kp = k_ref[0, prev, :].astype(BF16)
            vp = v_ref[0, prev, :].astype(BF16)
        out = jnp.zeros((Q_BLOCK, LANES), F32)
        lse_full = jnp.zeros((Q_BLOCK, LANES), F32)
        nt = (((1,), (1,)), ((), ()))
        for hm in head_masks:
            qh = jnp.where(hm, q, 0.0).astype(BF16)
            s_c = jnp.where(tri_cur, lax.dot_general(qh, kc, nt, preferred_element_type=F32), NEG)
            m = jnp.max(s_c, axis=-1, keepdims=True)
            if with_prev:
                s_p = jnp.where(tri_prev, lax.dot_general(qh, kp, nt, preferred_element_type=F32), NEG)
                m = jnp.maximum(m, jnp.max(s_p, axis=-1, keepdims=True))
            p_c = jnp.exp(s_c - m)
            den = jnp.sum(p_c, axis=-1, keepdims=True)
            o = _dot(p_c.astype(BF16), vc)
            if with_prev:
                p_p = jnp.exp(s_p - m)
                den = den + jnp.sum(p_p, axis=-1, keepdims=True)
                o = o + _dot(p_p.astype(BF16), vp)
            o = o * (1.0 / den)
            lse = m + jnp.log(den)
            out = jnp.where(hm, o, out)
            lse_full = jnp.where(hm, lse, lse_full)
        ob_scr[branch, cur, :] = out
        lse_scr[branch, cur, :] = lse_full

    for branch, (window, dil) in enumerate(BRANCHES):
        assert window // dil == Q_BLOCK
        sub_len = seq // dil
        n_blocks = sub_len // Q_BLOCK

        def first_body(r, carry, branch=branch, dil=dil):
            block(branch, dil, r, False)
            return carry

        lax.fori_loop(0, dil, first_body, 0)
        if n_blocks > 1:
            def rest_body(it, carry, branch=branch, dil=dil, n_blocks=n_blocks):
                r = it // (n_blocks - 1)
                i = it % (n_blocks - 1) + 1
                block(branch, dil, r + dil * Q_BLOCK * i, True)
                return carry

            lax.fori_loop(0, dil * (n_blocks - 1), rest_body, 0)

    rows_per_step = 256

    def merge_body(it, carry):
        sl = pl.ds(pl.multiple_of(it * rows_per_step, rows_per_step), rows_per_step)
        lses = [lse_scr[b, sl, :] for b in range(len(BRANCHES))]
        m = functools.reduce(jnp.maximum, lses)
        ws = [jnp.exp(l - m) for l in lses]
        num = sum(w * ob_scr[b, sl, :] for b, w in enumerate(ws))
        o_ref[0, sl, :] = (num * (1.0 / sum(ws))).astype(o_ref.dtype)
        return carry

    lax.fori_loop(0, seq // rows_per_step, merge_body, 0)


def _attn_prompt(q, k, v):
    b, seq, _ = q.shape
    spec = pl.BlockSpec((1, seq, LANES), lambda i, j: (i, 0, j))
    return pl.pallas_call(
        functools.partial(_attn_prompt_kernel, seq=seq),
        grid=(b, A_WIDTH // LANES),
        in_specs=[spec, spec, spec],
        out_specs=spec,
        out_shape=jax.ShapeDtypeStruct((b, seq, A_WIDTH), BF16),
        scratch_shapes=[pltpu.VMEM((len(BRANCHES), seq, LANES), F32),
                        pltpu.VMEM((len(BRANCHES), seq, LANES), F32)],
        compiler_params=pltpu.CompilerParams(dimension_semantics=("arbitrary", "arbitrary"),
                                             vmem_limit_bytes=VMEM_LIMIT),
        name="attn_prompt",
    )(q, k, v)


def _attn_sample_kernel(q_ref, kn_ref, vn_ref, kt_ref, vt_ref, o_ref, *, win_buf):
    back = win_buf - lax.broadcasted_iota(jnp.int32, (1, win_buf), 1)
    masks = [jnp.where((back & (dil - 1)) == 0, back, window + 1) <= window for window, dil in BRANCHES]
    for h in range(N_HEADS):
        q = q_ref[h]
        k_new = kn_ref[h]
        v_new = vn_ref[h]
        s = jnp.sum(kt_ref[h] * q, axis=0, keepdims=True)
        s_new = jnp.sum(q * k_new, axis=0, keepdims=True)
        ps, pns, lses = [], [], []
        for mask in masks:
            sb = jnp.where(mask, s, NEG)
            m = jnp.maximum(jnp.max(sb, axis=1, keepdims=True), s_new)
            p = jnp.exp(sb - m)
            pn = jnp.exp(s_new - m)
            den = jnp.sum(p, axis=1, keepdims=True) + pn
            inv = 1.0 / den
            ps.append(p * inv)
            pns.append(pn * inv)
            lses.append(m + jnp.log(den))
        mm = functools.reduce(jnp.maximum, lses)
        ws = [jnp.exp(l - mm) for l in lses]
        inv = 1.0 / sum(ws)
        p_all = sum(w * p for w, p in zip(ws, ps)) * inv
        pn_all = sum(w * pn for w, pn in zip(ws, pns)) * inv
        o_ref[h] = jnp.sum(vt_ref[h] * p_all, axis=1, keepdims=True) + pn_all * v_new


def _attn_sample(q, k_new, v_new, cache_kt, cache_vt, layer):
    n, win_buf = cache_kt.shape[1], cache_kt.shape[-1]
    for window, dil in BRANCHES:
        assert window <= win_buf <= PAST_LEN and dil & (dil - 1) == 0
    col = pl.BlockSpec((None, N_HEADS, HEAD_DIM, 1), lambda i: (i, 0, 0, 0))
    cache = pl.BlockSpec((None, None, N_HEADS, HEAD_DIM, win_buf), lambda i: (layer, i, 0, 0, 0))
    as_cols = lambda a: a.reshape(n, N_HEADS, HEAD_DIM, 1)
    return pl.pallas_call(
        functools.partial(_attn_sample_kernel, win_buf=win_buf),
        grid=(n,),
        in_specs=[col, col, col, cache, cache],
        out_specs=col,
        out_shape=jax.ShapeDtypeStruct((n, N_HEADS, HEAD_DIM, 1), F32),
        compiler_params=pltpu.CompilerParams(dimension_semantics=("arbitrary",), vmem_limit_bytes=VMEM_LIMIT),
        name="attn_sample",
    )(as_cols(q), as_cols(k_new), as_cols(v_new), cache_kt, cache_vt).reshape(n, A_WIDTH)


def _mix_ffn_prompt_kernel(h_ref, att_ref, u_ref, vb_ref, wsp_ref, bsp_ref, wout_ref, gffn_ref, wup_ref, wdn_ref,
                           o_ref, gate_scr, *, tm):
    lane = lax.broadcasted_iota(jnp.int32, (1, LANES), 1)
    lo = lane < B_WIDTH // N_GROUPS
    ri = lax.broadcasted_iota(jnp.int32, (CHUNK, 2 * CHUNK), 0)
    ci = lax.broadcasted_iota(jnp.int32, (CHUNK, 2 * CHUNK), 1)
    causal = (ci & (CHUNK - 1)) <= ri
    n_pairs = B_WIDTH // LANES
    w_pairs = [jnp.where(causal, wsp_ref[p], 0.0).astype(BF16) for p in range(n_pairs)]
    for c in range(tm // CHUNK):
        rows = slice(c * CHUNK, (c + 1) * CHUNK)
        for p in range(n_pairs):
            sl = slice(p * LANES, (p + 1) * LANES)
            slab = vb_ref[rows, sl]
            rhs = jnp.concatenate([jnp.where(lo, slab, 0.0), jnp.where(lo, 0.0, slab)], axis=0).astype(BF16)
            mixed = _dot(w_pairs[p], rhs) + bsp_ref[:, sl]
            gate_scr[rows, sl] = (u_ref[rows, sl] * mixed).astype(BF16)
    mix = _dot(att_ref[...], wout_ref[0:A_WIDTH, :]) + _dot(gate_scr[...], wout_ref[A_WIDTH:, :])
    h1 = h_ref[...] + mix
    o_ref[...] = _ffn(h1, gffn_ref[...], wup_ref, wdn_ref)


def _mix_ffn_sample_kernel(h_ref, att_ref, u_ref, vb_ref, w0_ref, b0_ref, wout_ref, gffn_ref, wup_ref, wdn_ref,
                           o_ref):
    gate = (u_ref[...] * (vb_ref[...] * w0_ref[...] + b0_ref[...])).astype(BF16)
    mix = _dot(att_ref[...].astype(BF16), wout_ref[0:A_WIDTH, :]) + _dot(gate, wout_ref[A_WIDTH:, :])
    h1 = h_ref[...] + mix
    o_ref[...] = _ffn(h1, gffn_ref[...], wup_ref, wdn_ref)


def _mix_ffn(h, att, u, vb, gate_w, gate_b, wout_bf, gffn, wup_bf, wdn_bf, tm, sample):
    t = h.shape[0]
    row = lambda i: (i, 0)
    if sample:
        kern = _mix_ffn_sample_kernel
        scratch = []
    else:
        kern = functools.partial(_mix_ffn_prompt_kernel, tm=tm)
        scratch = [pltpu.VMEM((tm, B_WIDTH), BF16)]
    return pl.pallas_call(
        kern,
        grid=(t // tm,),
        in_specs=[
            pl.BlockSpec((tm, D_MODEL), row), pl.BlockSpec((tm, A_WIDTH), row),
            pl.BlockSpec((tm, B_WIDTH), row), pl.BlockSpec((tm, B_WIDTH), row),
            _const_spec(gate_w.shape), _const_spec(gate_b.shape),
            _const_spec((A_WIDTH + B_WIDTH, D_MODEL)), _const_spec((1, D_MODEL)),
            _const_spec((D_MODEL, D_FF)), _const_spec((D_FF, D_MODEL)),
        ],
        out_specs=pl.BlockSpec((tm, D_MODEL), row),
        out_shape=jax.ShapeDtypeStruct((t, D_MODEL), F32),
        scratch_shapes=scratch,
        compiler_params=pltpu.CompilerParams(dimension_semantics=("arbitrary",), vmem_limit_bytes=VMEM_LIMIT),
        name="mix_ffn_sample" if sample else "mix_ffn",
    )(h, att, u, vb, gate_w, gate_b, wout_bf, gffn, wup_bf, wdn_bf)


def _glu_in(h, gmix, win_ref):
    n = _rmsnorm(h, gmix).astype(BF16)
    a = _dot(n, win_ref[:, 0:D_MODEL])
    gate = _dot(n, win_ref[:, D_MODEL:])
    return a * jax.nn.sigmoid(gate)


def _conv_tail(h, y, cg_ref, cb_ref, wout_ref, gffn_ref, wup_ref, wdn_ref):
    z = _layernorm(y, cg_ref[...], cb_ref[...])
    z = (z * jax.nn.sigmoid(z)).astype(BF16)
    h1 = h + _dot(z, wout_ref[...])
    return _ffn(h1, gffn_ref[...], wup_ref, wdn_ref)


def _conv_ffn_prompt_kernel(h_ref, gmix_ref, win_ref, wdw_ref, bdw_ref, cg_ref, cb_ref, wout_ref, gffn_ref,
                            wup_ref, wdn_ref, o_ref, cc_ref, xbuf, *, tm):
    s = pl.program_id(1)
    keep = CONV_WIDTH - 1

    @pl.when(s == 0)
    def _():
        xbuf[0:HALO, :] = jnp.zeros((HALO, D_MODEL), F32)

    h = h_ref[...]
    xbuf[HALO:HALO + tm, :] = _glu_in(h, gmix_ref[...], win_ref)
    y = jnp.broadcast_to(bdw_ref[...], (tm, D_MODEL))
    for w in range(CONV_WIDTH):
        off = HALO - keep + w
        y = y + xbuf[off:off + tm, :] * wdw_ref[w:w + 1, :]

    @pl.when(s == pl.num_programs(1) - 1)
    def _():
        cc_ref[0] = xbuf[HALO + tm - keep:HALO + tm, :]

    xbuf[0:HALO, :] = xbuf[tm:tm + HALO, :]
    o_ref[...] = _conv_tail(h, y, cg_ref, cb_ref, wout_ref, gffn_ref, wup_ref, wdn_ref)


def _conv_ffn_sample_kernel(h_ref, gmix_ref, win_ref, wdw_ref, bdw_ref, cg_ref, cb_ref, wout_ref, gffn_ref,
                            wup_ref, wdn_ref, st_ref, o_ref, cc_ref):
    keep = CONV_WIDTH - 1
    h = h_ref[...]
    x = _glu_in(h, gmix_ref[...], win_ref)
    y = x * wdw_ref[keep:keep + 1, :] + bdw_ref[...]
    for w in range(keep):
        y = y + st_ref[w] * wdw_ref[w:w + 1, :]
    for w in range(keep - 1):
        cc_ref[w] = st_ref[w + 1]
    cc_ref[keep - 1] = x
    o_ref[...] = _conv_tail(h, y, cg_ref, cb_ref, wout_ref, gffn_ref, wup_ref, wdn_ref)


def _conv_weight_specs():
    return [
        _const_spec((1, D_MODEL)), _const_spec((D_MODEL, 2 * D_MODEL)), _const_spec((CONV_WIDTH, D_MODEL)),
        _const_spec((1, D_MODEL)), _const_spec((1, D_MODEL)), _const_spec((1, D_MODEL)),
        _const_spec((D_MODEL, D_MODEL)), _const_spec((1, D_MODEL)),
        _const_spec((D_MODEL, D_FF)), _const_spec((D_FF, D_MODEL)),
    ]


def _conv_ffn_prompt(h, weights, batch, seq, tm):
    keep = CONV_WIDTH - 1
    n_s = seq // tm
    row = lambda b, s: (b * n_s + s, 0)
    return pl.pallas_call(
        functools.partial(_conv_ffn_prompt_kernel, tm=tm),
        grid=(batch, n_s),
        in_specs=[pl.BlockSpec((tm, D_MODEL), row)] + _conv_weight_specs(),
        out_specs=[pl.BlockSpec((tm, D_MODEL), row), pl.BlockSpec((1, keep, D_MODEL), lambda b, s: (b, 0, 0))],
        out_shape=[jax.ShapeDtypeStruct((batch * seq, D_MODEL), F32),
                   jax.ShapeDtypeStruct((batch, keep, D_MODEL), F32)],
        scratch_shapes=[pltpu.VMEM((HALO + tm, D_MODEL), F32)],
        compiler_params=pltpu.CompilerParams(dimension_semantics=("arbitrary", "arbitrary"),
                                             vmem_limit_bytes=VMEM_LIMIT),
        name="conv_ffn",
    )(h, *weights)


def _conv_ffn_sample(h, weights, state_t, layer):
    n = h.shape[0]
    keep = CONV_WIDTH - 1
    return pl.pallas_call(
        _conv_ffn_sample_kernel,
        grid=(1,),
        in_specs=[_const_spec((n, D_MODEL))] + _conv_weight_specs()
        + [pl.BlockSpec((None, keep, n, D_MODEL), lambda i: (layer, 0, 0, 0))],
        out_specs=[_const_spec((n, D_MODEL)), _const_spec((keep, n, D_MODEL))],
        out_shape=[jax.ShapeDtypeStruct((n, D_MODEL), F32), jax.ShapeDtypeStruct((keep, n, D_MODEL), F32)],
        compiler_params=pltpu.CompilerParams(dimension_semantics=("arbitrary",), vmem_limit_bytes=VMEM_LIMIT),
        name="conv_ffn_sample",
    )(h, *weights, state_t)


def _rope_tables(pos):
    inv = ROPE_THETA ** (-jnp.arange(0, HEAD_DIM, 2, dtype=F32) / HEAD_DIM)
    ang = pos.astype(F32)[:, None] * inv[None, :]
    cos, sin = jnp.cos(ang), jnp.sin(ang)
    reps = LANES // HEAD_DIM
    return jnp.tile(jnp.concatenate([cos, cos], axis=-1), (1, reps)), jnp.tile(jnp.concatenate([-sin, sin], axis=-1), (1, reps))


def kernel(x_prompt, x_sample, cache_a_k, cache_a_v, state_c_conv, norm_mix_g, norm_ffn_g, w_ffn_up, w_ffn_down, w_in_ab, q_norm_g, k_norm_g, vb_norm_g, vb_norm_b, w_spatial, b_spatial, w_out_ab, w_c_in, w_c_dw, b_c_dw, c_norm_g, c_norm_b, w_c_out):
    batch, seq, _ = x_prompt.shape
    n_dec, dec_seq, _ = x_sample.shape
    depth = norm_mix_g.shape[0]
    n_ab, _, win_buf = cache_a_k.shape[:3]
    past_len = PAST_LEN
    assert dec_seq == 1 and seq % (BRANCHES[-1][1] * Q_BLOCK) == 0 and seq % CHUNK == 0
    tm = 256

    hp = x_prompt.reshape(batch * seq, D_MODEL)
    hs = x_sample.reshape(n_dec * dec_seq, D_MODEL)
    cos_p, sin_p = _rope_tables(jnp.arange(seq, dtype=jnp.int32))
    cos_s, sin_s = _rope_tables(jnp.full((n_dec,), past_len, dtype=jnp.int32))
    head_id = jnp.arange(LANES) // HEAD_DIM
    head_mean = jnp.where(head_id[:, None] == head_id[None, :], 1.0 / HEAD_DIM, 0.0).astype(BF16)
    cache_kt = cache_a_k.transpose(0, 1, 3, 4, 2)
    cache_vt = cache_a_v.transpose(0, 1, 3, 4, 2)
    state_t = state_c_conv.transpose(0, 2, 1, 3)
    group_dim = B_WIDTH // N_GROUPS
    row2 = lambda a: a.reshape(1, -1)

    ak_p, av_p, ak_s, av_s, bv_p, bv_s, cc_p, cc_s = [], [], [], [], [], [], [], []
    for layer in range(depth):
        j = layer // 2
        gffn = row2(norm_ffn_g[layer])
        wup = w_ffn_up[layer].astype(BF16)
        wdn = w_ffn_down[layer].astype(BF16)
        gmix = row2(norm_mix_g[layer])
        if layer % 2 == 0:
            w_in = w_in_ab[j].astype(BF16)
            wout = w_out_ab[j].astype(BF16)
            qg = row2(jnp.tile(q_norm_g[j], N_HEADS))
            kg = row2(jnp.tile(k_norm_g[j], N_HEADS))
            vbg, vbb = row2(vb_norm_g[j]), row2(vb_norm_b[j])
            qp, kp, vp, up, vbp = _ab_in(hp, gmix, w_in, cos_p, sin_p, qg, kg, head_mean, vbg, vbb, tm)
            qs, ks, vs, us, vbs = _ab_in(hs, gmix, w_in, cos_s, sin_s, qg, kg, head_mean, vbg, vbb, n_dec)
            shp = (batch, seq, A_WIDTH)
            att_p = _attn_prompt(qp.reshape(shp), kp.reshape(shp), vp.reshape(shp)).reshape(batch * seq, A_WIDTH)
            att_s = _attn_sample(qs, ks, vs, cache_kt, cache_vt, j)
            w_pairs = w_spatial[j].reshape(N_GROUPS // 2, 2, CHUNK, CHUNK).transpose(0, 2, 1, 3).reshape(N_GROUPS // 2, CHUNK, 2 * CHUNK)
            b_rows = jnp.repeat(b_spatial[j].T, group_dim, axis=1)
            w0 = row2(jnp.repeat(w_spatial[j][:, 0, 0], group_dim))
            b0 = row2(jnp.repeat(b_spatial[j][:, 0], group_dim))
            hp = _mix_ffn(hp, att_p, up, vbp, w_pairs, b_rows, wout, gffn, wup, wdn, tm, False)
            hs = _mix_ffn(hs, att_s, us, vbs, w0, b0, wout, gffn, wup, wdn, n_dec, True)
            prompt_buf = min(BRANCHES[-1][0], seq)
            ak_p.append(kp.reshape(batch, seq, N_HEADS, HEAD_DIM)[:, seq - prompt_buf:])
            av_p.append(vp.reshape(batch, seq, N_HEADS, HEAD_DIM)[:, seq - prompt_buf:])
            ak_s.append(ks.reshape(n_dec, dec_seq, N_HEADS, HEAD_DIM))
            av_s.append(vs.reshape(n_dec, dec_seq, N_HEADS, HEAD_DIM))
            last_chunk_start = ((seq - 1) // CHUNK) * CHUNK
            bv_p.append(vbp.reshape(batch, seq, B_WIDTH)[:, last_chunk_start:])
            bv_s.append(vbs.reshape(n_dec, dec_seq, B_WIDTH))
        else:
            weights = (gmix, w_c_in[j].astype(BF16), w_c_dw[j], row2(b_c_dw[j]), row2(c_norm_g[j]),
                       row2(c_norm_b[j]), w_c_out[j].astype(BF16), gffn, wup, wdn)
            hp, new_cp = _conv_ffn_prompt(hp, weights, batch, seq, tm)
            hs, new_cs = _conv_ffn_sample(hs, weights, state_t, j)
            cc_p.append(new_cp)
            cc_s.append(new_cs)

    return (hp.reshape(batch, seq, D_MODEL), hs.reshape(n_dec, dec_seq, D_MODEL),
            jnp.stack(ak_p), jnp.stack(av_p), jnp.stack(ak_s), jnp.stack(av_s),
            jnp.stack(bv_p), jnp.stack(bv_s), jnp.stack(cc_p), jnp.stack(cc_s).transpose(0, 2, 1, 3))
```

```python
import functools

import jax
import jax.numpy as jnp
from jax import lax
from jax.experimental import pallas as pl
from jax.experimental.pallas import tpu as pltpu

F32 = jnp.float32
BF16 = jnp.bfloat16

D_MODEL = 1024
N_HEADS = 8
HEAD_DIM = 64
A_WIDTH = N_HEADS * HEAD_DIM
B_WIDTH = 512
N_GROUPS = 8
CHUNK = 128
Q_BLOCK = 128
BRANCHES = ((128, 1), (512, 4), (2048, 16))
ROPE_THETA = 10000.0
CONV_WIDTH = 31
PAST_LEN = 8192
D_FF = 4 * D_MODEL
EPS = 1e-6
IN_AB = 3 * A_WIDTH + 2 * B_WIDTH

LANES = 128
SUBLANES = 8
HALO = 32
CONV_ROWS = 64
FF_CHUNK = 1024
NEG = -1e30
LOG2_E = 1.4426950408889634
QK_SCALE = HEAD_DIM ** -0.5 * LOG2_E
ATTN_GROUP = 4
VMEM_LIMIT = 56 * 1024 * 1024


def _rmsnorm(x, g):
    ms = jnp.mean(x * x, axis=-1, keepdims=True)
    return x * lax.rsqrt(ms + EPS) * g


def _layernorm(x, g, b):
    mu = jnp.mean(x, axis=-1, keepdims=True)
    xc = x - mu
    var = jnp.mean(xc * xc, axis=-1, keepdims=True)
    return xc * lax.rsqrt(var + EPS) * g + b


def _dot(a, b):
    return jnp.dot(a, b, preferred_element_type=F32)


def _ffn(h1, g, wup_ref, wdn_ref):
    n = _rmsnorm(h1, g).astype(BF16)
    acc = h1
    for c in range(D_FF // FF_CHUNK):
        a = _dot(n, wup_ref[:, c * FF_CHUNK:(c + 1) * FF_CHUNK])
        a = jnp.maximum(a, 0.0)
        a = (a * a).astype(BF16)
        acc = acc + _dot(a, wdn_ref[c * FF_CHUNK:(c + 1) * FF_CHUNK, :])
    return acc


def _ab_in_kernel(h_ref, g_ref, w_ref, cos_ref, sin_ref, qg_ref, kg_ref, hm_ref, vbg_ref, vbb_ref,
                  q_ref, k_ref, v_ref, u_ref, vb_ref):
    n = _rmsnorm(h_ref[...], g_ref[...]).astype(BF16)
    cos = cos_ref[...]
    sin = sin_ref[...]
    hm = hm_ref[...]
    lane = lax.broadcasted_iota(jnp.int32, (1, LANES), 1)
    first_half = (lane & (HEAD_DIM // 2)) == 0

    def head_norm_rope(z, gain_ref, out_ref, scale):
        for c in range(A_WIDTH // LANES):
            sl = slice(c * LANES, (c + 1) * LANES)
            zc = z[:, sl]
            sq = zc * zc
            hi = sq.astype(BF16)
            lo = (sq - hi.astype(F32)).astype(BF16)
            ms = _dot(hi, hm) + _dot(lo, hm)
            y = zc * lax.rsqrt(ms + EPS) * gain_ref[:, sl]
            partner = jnp.where(first_half, pltpu.roll(y, LANES - HEAD_DIM // 2, 1),
                                pltpu.roll(y, HEAD_DIM // 2, 1))
            out = y * cos + partner * sin
            if scale != 1.0:
                out = out * scale
            out_ref[:, sl] = out

    head_norm_rope(_dot(n, w_ref[:, 0:A_WIDTH]), qg_ref, q_ref, QK_SCALE)
    head_norm_rope(_dot(n, w_ref[:, A_WIDTH:2 * A_WIDTH]), kg_ref, k_ref, 1.0)
    v_ref[...] = _dot(n, w_ref[:, 2 * A_WIDTH:3 * A_WIDTH])
    u_ref[...] = jax.nn.gelu(_dot(n, w_ref[:, 3 * A_WIDTH:3 * A_WIDTH + B_WIDTH]))
    vb = jax.nn.gelu(_dot(n, w_ref[:, 3 * A_WIDTH + B_WIDTH:IN_AB]))
    vb_ref[...] = _layernorm(vb, vbg_ref[...], vbb_ref[...])


def _const_spec(shape):
    nd = len(shape)
    return pl.BlockSpec(shape, lambda *_: (0,) * nd, pipeline_mode=pl.Buffered(1))


def _ab_in(h, g, w_bf, cos_t, sin_t, qg, kg, hm, vbg, vbb, tm):
    t = h.shape[0]
    n_pos_blocks = cos_t.shape[0] // tm
    row = lambda i: (i, 0)
    tab = lambda i: (i % n_pos_blocks, 0)
    out = jax.ShapeDtypeStruct((t, A_WIDTH), F32)
    return pl.pallas_call(
        _ab_in_kernel,
        grid=(t // tm,),
        in_specs=[
            pl.BlockSpec((tm, D_MODEL), row), _const_spec((1, D_MODEL)), _const_spec((D_MODEL, IN_AB)),
            pl.BlockSpec((tm, LANES), tab), pl.BlockSpec((tm, LANES), tab),
            _const_spec((1, A_WIDTH)), _const_spec((1, A_WIDTH)), _const_spec((LANES, LANES)),
            _const_spec((1, B_WIDTH)), _const_spec((1, B_WIDTH)),
        ],
        out_specs=[pl.BlockSpec((tm, A_WIDTH), row)] * 5,
        out_shape=[out] * 5,
        compiler_params=pltpu.CompilerParams(dimension_semantics=("arbitrary",), vmem_limit_bytes=VMEM_LIMIT),
        name="ab_in",
    )(h, g, w_bf, cos_t, sin_t, qg, kg, hm, vbg, vbb)


def _attn_prompt_kernel(q_ref, k_ref, v_ref, o_ref, ob_scr, lse_scr, *, seq):
    lane = lax.broadcasted_iota(jnp.int32, (1, LANES), 1)
    head0 = lane < HEAD_DIM
    ri = lax.broadcasted_iota(jnp.int32, (2 * Q_BLOCK, 2 * Q_BLOCK), 0) & (Q_BLOCK - 1)
    ci = lax.broadcasted_iota(jnp.int32, (2 * Q_BLOCK, 2 * Q_BLOCK), 1)
    band_two = jnp.where(ci < Q_BLOCK, ci - ri, ri + Q_BLOCK - ci) >= 0
    band_one = (lax.broadcasted_iota(jnp.int32, (2 * Q_BLOCK, Q_BLOCK), 1)
                <= lax.broadcasted_iota(jnp.int32, (2 * Q_BLOCK, Q_BLOCK), 0) & (Q_BLOCK - 1))

    def rows(start, dil):
        if dil == 1:
            return pl.ds(start if isinstance(start, int) else pl.multiple_of(start, Q_BLOCK), Q_BLOCK)
        return pl.ds(start, Q_BLOCK, stride=dil)

    def load(dil, start, with_prev):
        cur = rows(start, dil)
        q = q_ref[0, cur, :]
        q2 = jnp.concatenate([jnp.where(head0, q, 0.0), jnp.where(head0, 0.0, q)], axis=0).astype(BF16)
        k = k_ref[0, cur, :].astype(BF16)
        v = v_ref[0, cur, :].astype(BF16)
        if with_prev:
            prev = rows(start - dil * Q_BLOCK, dil)
            k = jnp.concatenate([k_ref[0, prev, :].astype(BF16), k], axis=0)
            v = jnp.concatenate([v_ref[0, prev, :].astype(BF16), v], axis=0)
        return q2, k, v

    def attend(q2, k, v, with_prev):
        n_keys = k.shape[0]
        s = lax.dot_general(q2, k, (((1,), (1,)), ((), ())), preferred_element_type=F32)
        s = jnp.where(band_two if with_prev else band_one, s, NEG)
        m = jnp.max(s, axis=-1, keepdims=True)
        p = jnp.exp2(s - m).astype(BF16)
        r = _dot(p, jnp.concatenate([v, jnp.ones((n_keys, LANES), BF16)], axis=1))
        den = r[:, LANES:]
        o = r[:, :LANES] * (1.0 / den)
        lse = m + jnp.log2(den)
        return jnp.where(head0, o[:Q_BLOCK], o[Q_BLOCK:]), jnp.where(head0, lse[:Q_BLOCK], lse[Q_BLOCK:])

    def blocks(branch, dil, starts, with_prev):
        loaded = [load(dil, st, with_prev) for st in starts]
        results = [attend(*ld, with_prev) for ld in loaded]
        for st, (o, lse) in zip(starts, results):
            ob_scr[branch, rows(st, dil), :] = o
            lse_scr[branch, rows(st, dil), :] = lse

    def grouped_loop(total, body):
        n_full = total // ATTN_GROUP

        def loop_body(it, carry):
            body([it * ATTN_GROUP + g for g in range(ATTN_GROUP)])
            return carry

        if n_full:
            lax.fori_loop(0, n_full, loop_body, 0)
        if total % ATTN_GROUP:
            body(list(range(n_full * ATTN_GROUP, total)))

    for branch, (window, dil) in enumerate(BRANCHES):
        assert window // dil == Q_BLOCK
        sub_len = seq // dil
        n_blocks = sub_len // Q_BLOCK

        def first_body(items, branch=branch, dil=dil):
            blocks(branch, dil, items, False)

        grouped_loop(dil, first_body)
        if n_blocks > 1:
            def rest_body(items, branch=branch, dil=dil, n_blocks=n_blocks):
                starts = [it // (n_blocks - 1) + dil * Q_BLOCK * (it % (n_blocks - 1) + 1) for it in items]
                blocks(branch, dil, starts, True)

            grouped_loop(dil * (n_blocks - 1), rest_body)

    rows_per_step = 256

    def merge_body(it, carry):
        sl = pl.ds(pl.multiple_of(it * rows_per_step, rows_per_step), rows_per_step)
        lses = [lse_scr[b, sl, :] for b in range(len(BRANCHES))]
        m = functools.reduce(jnp.maximum, lses)
        ws = [jnp.exp2(l - m) for l in lses]
        num = sum(w * ob_scr[b, sl, :] for b, w in enumerate(ws))
        o_ref[0, sl, :] = (num * (1.0 / sum(ws))).astype(o_ref.dtype)
        return carry

    lax.fori_loop(0, seq // rows_per_step, merge_body, 0)


def _attn_prompt(q, k, v):
    b, seq, _ = q.shape
    spec = pl.BlockSpec((1, seq, LANES), lambda i, j: (i, 0, j))
    return pl.pallas_call(
        functools.partial(_attn_prompt_kernel, seq=seq),
        grid=(b, A_WIDTH // LANES),
        in_specs=[spec, spec, spec],
        out_specs=spec,
        out_shape=jax.ShapeDtypeStruct((b, seq, A_WIDTH), BF16),
        scratch_shapes=[pltpu.VMEM((len(BRANCHES), seq, LANES), F32),
                        pltpu.VMEM((len(BRANCHES), seq, LANES), F32)],
        compiler_params=pltpu.CompilerParams(dimension_semantics=("arbitrary", "arbitrary"),
                                             vmem_limit_bytes=VMEM_LIMIT),
        name="attn_prompt",
    )(q, k, v)


def _attn_sample_kernel(q_ref, kn_ref, vn_ref, kt_ref, vt_ref, o_ref, *, win_buf):
    back = win_buf - lax.broadcasted_iota(jnp.int32, (1, win_buf), 1)
    masks = [jnp.where((back & (dil - 1)) == 0, back, window + 1) <= window for window, dil in BRANCHES]
    for h in range(N_HEADS):
        q = q_ref[h]
        k_new = kn_ref[h]
        v_new = vn_ref[h]
        s = jnp.sum(kt_ref[h] * q, axis=0, keepdims=True)
        s_new = jnp.sum(q * k_new, axis=0, keepdims=True)
        ps, pns, lses = [], [], []
        for mask in masks:
            sb = jnp.where(mask, s, NEG)
            m = jnp.maximum(jnp.max(sb, axis=1, keepdims=True), s_new)
            p = jnp.exp2(sb - m)
            pn = jnp.exp2(s_new - m)
            den = jnp.sum(p, axis=1, keepdims=True) + pn
            inv = 1.0 / den
            ps.append(p * inv)
            pns.append(pn * inv)
            lses.append(m + jnp.log2(den))
        mm = functools.reduce(jnp.maximum, lses)
        ws = [jnp.exp2(l - mm) for l in lses]
        inv = 1.0 / sum(ws)
        p_all = sum(w * p for w, p in zip(ws, ps)) * inv
        pn_all = sum(w * pn for w, pn in zip(ws, pns)) * inv
        o_ref[h] = jnp.sum(vt_ref[h] * p_all, axis=1, keepdims=True) + pn_all * v_new


def _attn_sample(q, k_new, v_new, cache_kt, cache_vt, layer):
    n, win_buf = cache_kt.shape[1], cache_kt.shape[-1]
    for window, dil in BRANCHES:
        assert window <= win_buf <= PAST_LEN and dil & (dil - 1) == 0
    col = pl.BlockSpec((None, N_HEADS, HEAD_DIM, 1), lambda i: (i, 0, 0, 0))
    cache = pl.BlockSpec((None, None, N_HEADS, HEAD_DIM, win_buf), lambda i: (layer, i, 0, 0, 0))
    as_cols = lambda a: a.reshape(n, N_HEADS, HEAD_DIM, 1)
    return pl.pallas_call(
        functools.partial(_attn_sample_kernel, win_buf=win_buf),
        grid=(n,),
        in_specs=[col, col, col, cache, cache],
        out_specs=col,
        out_shape=jax.ShapeDtypeStruct((n, N_HEADS, HEAD_DIM, 1), F32),
        compiler_params=pltpu.CompilerParams(dimension_semantics=("arbitrary",), vmem_limit_bytes=VMEM_LIMIT),
        name="attn_sample",
    )(as_cols(q), as_cols(k_new), as_cols(v_new), cache_kt, cache_vt).reshape(n, A_WIDTH)


def _mix_ffn_prompt_kernel(h_ref, att_ref, u_ref, vb_ref, wsp_ref, bsp_ref, wout_ref, gffn_ref, wup_ref, wdn_ref,
                           o_ref, gate_scr, *, tm):
    lane = lax.broadcasted_iota(jnp.int32, (1, LANES), 1)
    lo = lane < B_WIDTH // N_GROUPS
    ri = lax.broadcasted_iota(jnp.int32, (CHUNK, 2 * CHUNK), 0)
    ci = lax.broadcasted_iota(jnp.int32, (CHUNK, 2 * CHUNK), 1)
    causal = (ci & (CHUNK - 1)) <= ri
    n_pairs = B_WIDTH // LANES
    w_pairs = [jnp.where(causal, wsp_ref[p], 0.0).astype(BF16) for p in range(n_pairs)]
    for c in range(tm // CHUNK):
        rows = slice(c * CHUNK, (c + 1) * CHUNK)
        for p in range(n_pairs):
            sl = slice(p * LANES, (p + 1) * LANES)
            slab = vb_ref[rows, sl]
            rhs = jnp.concatenate([jnp.where(lo, slab, 0.0), jnp.where(lo, 0.0, slab)], axis=0).astype(BF16)
            mixed = _dot(w_pairs[p], rhs) + bsp_ref[:, sl]
            gate_scr[rows, sl] = (u_ref[rows, sl] * mixed).astype(BF16)
    mix = _dot(att_ref[...], wout_ref[0:A_WIDTH, :]) + _dot(gate_scr[...], wout_ref[A_WIDTH:, :])
    h1 = h_ref[...] + mix
    o_ref[...] = _ffn(h1, gffn_ref[...], wup_ref, wdn_ref)


def _mix_ffn_sample_kernel(h_ref, att_ref, u_ref, vb_ref, w0_ref, b0_ref, wout_ref, gffn_ref, wup_ref, wdn_ref,
                           o_ref):
    gate = (u_ref[...] * (vb_ref[...] * w0_ref[...] + b0_ref[...])).astype(BF16)
    mix = _dot(att_ref[...].astype(BF16), wout_ref[0:A_WIDTH, :]) + _dot(gate, wout_ref[A_WIDTH:, :])
    h1 = h_ref[...] + mix
    o_ref[...] = _ffn(h1, gffn_ref[...], wup_ref, wdn_ref)


def _mix_ffn(h, att, u, vb, gate_w, gate_b, wout_bf, gffn, wup_bf, wdn_bf, tm, sample):
    t = h.shape[0]
    row = lambda i: (i, 0)
    if sample:
        kern = _mix_ffn_sample_kernel
        scratch = []
    else:
        kern = functools.partial(_mix_ffn_prompt_kernel, tm=tm)
        scratch = [pltpu.VMEM((tm, B_WIDTH), BF16)]
    return pl.pallas_call(
        kern,
        grid=(t // tm,),
        in_specs=[
            pl.BlockSpec((tm, D_MODEL), row), pl.BlockSpec((tm, A_WIDTH), row),
            pl.BlockSpec((tm, B_WIDTH), row), pl.BlockSpec((tm, B_WIDTH), row),
            _const_spec(gate_w.shape), _const_spec(gate_b.shape),
            _const_spec((A_WIDTH + B_WIDTH, D_MODEL)), _const_spec((1, D_MODEL)),
            _const_spec((D_MODEL, D_FF)), _const_spec((D_FF, D_MODEL)),
        ],
        out_specs=pl.BlockSpec((tm, D_MODEL), row),
        out_shape=jax.ShapeDtypeStruct((t, D_MODEL), F32),
        scratch_shapes=scratch,
        compiler_params=pltpu.CompilerParams(dimension_semantics=("arbitrary",), vmem_limit_bytes=VMEM_LIMIT),
        name="mix_ffn_sample" if sample else "mix_ffn",
    )(h, att, u, vb, gate_w, gate_b, wout_bf, gffn, wup_bf, wdn_bf)


def _glu_in(h, gmix, win_ref):
    n = _rmsnorm(h, gmix).astype(BF16)
    a = _dot(n, win_ref[:, 0:D_MODEL])
    gate = _dot(n, win_ref[:, D_MODEL:])
    return a * jax.nn.sigmoid(gate)


def _conv_tail(h, y, cg_ref, cb_ref, wout_ref, gffn_ref, wup_ref, wdn_ref):
    z = _layernorm(y, cg_ref[...], cb_ref[...])
    z = (z * jax.nn.sigmoid(z)).astype(BF16)
    h1 = h + _dot(z, wout_ref[...])
    return _ffn(h1, gffn_ref[...], wup_ref, wdn_ref)


def _conv_ffn_prompt_kernel(h_ref, gmix_ref, win_ref, wdw_ref, bdw_ref, cg_ref, cb_ref, wout_ref, gffn_ref,
                            wup_ref, wdn_ref, o_ref, cc_ref, xbuf, xs_scr, y_scr, *, tm):
    s = pl.program_id(1)
    keep = CONV_WIDTH - 1

    @pl.when(s == 0)
    def _():
        xbuf[0:HALO, :] = jnp.zeros((HALO, D_MODEL), F32)

    h = h_ref[...]
    xbuf[HALO:HALO + tm, :] = _glu_in(h, gmix_ref[...], win_ref)

    shift_rows = tm + HALO - SUBLANES
    for b in range(1, SUBLANES):
        xs_scr[b - 1] = xbuf[b:b + shift_rows, :]

    def row_block(rb, carry):
        r0 = pl.multiple_of(rb * CONV_ROWS, CONV_ROWS)
        for col in range(D_MODEL // LANES):
            cs = slice(col * LANES, (col + 1) * LANES)
            acc = jnp.broadcast_to(bdw_ref[:, cs], (CONV_ROWS, LANES))
            for w in range(CONV_WIDTH):
                a, b = divmod(HALO - keep + w, SUBLANES)
                rows = pl.ds(r0 + SUBLANES * a, CONV_ROWS)
                x = xbuf[rows, cs] if b == 0 else xs_scr[b - 1, rows, cs]
                acc = acc + x * wdw_ref[w:w + 1, cs]
            y_scr[pl.ds(r0, CONV_ROWS), cs] = acc
        return carry

    lax.fori_loop(0, tm // CONV_ROWS, row_block, 0)

    @pl.when(s == pl.num_programs(1) - 1)
    def _():
        cc_ref[0] = xbuf[HALO + tm - keep:HALO + tm, :]

    xbuf[0:HALO, :] = xbuf[tm:tm + HALO, :]
    o_ref[...] = _conv_tail(h, y_scr[...], cg_ref, cb_ref, wout_ref, gffn_ref, wup_ref, wdn_ref)


def _conv_ffn_sample_kernel(h_ref, gmix_ref, win_ref, wdw_ref, bdw_ref, cg_ref, cb_ref, wout_ref, gffn_ref,
                            wup_ref, wdn_ref, st_ref, o_ref, cc_ref):
    keep = CONV_WIDTH - 1
    h = h_ref[...]
    x = _glu_in(h, gmix_ref[...], win_ref)
    y = x * wdw_ref[keep:keep + 1, :] + bdw_ref[...]
    for w in range(keep):
        y = y + st_ref[w] * wdw_ref[w:w + 1, :]
    for w in range(keep - 1):
        cc_ref[w] = st_ref[w + 1]
    cc_ref[keep - 1] = x
    o_ref[...] = _conv_tail(h, y, cg_ref, cb_ref, wout_ref, gffn_ref, wup_ref, wdn_ref)


def _conv_weight_specs():
    return [
        _const_spec((1, D_MODEL)), _const_spec((D_MODEL, 2 * D_MODEL)), _const_spec((CONV_WIDTH, D_MODEL)),
        _const_spec((1, D_MODEL)), _const_spec((1, D_MODEL)), _const_spec((1, D_MODEL)),
        _const_spec((D_MODEL, D_MODEL)), _const_spec((1, D_MODEL)),
        _const_spec((D_MODEL, D_FF)), _const_spec((D_FF, D_MODEL)),
    ]


def _conv_ffn_prompt(h, weights, batch, seq, tm):
    keep = CONV_WIDTH - 1
    n_s = seq // tm
    row = lambda b, s: (b * n_s + s, 0)
    return pl.pallas_call(
        functools.partial(_conv_ffn_prompt_kernel, tm=tm),
        grid=(batch, n_s),
        in_specs=[pl.BlockSpec((tm, D_MODEL), row)] + _conv_weight_specs(),
        out_specs=[pl.BlockSpec((tm, D_MODEL), row), pl.BlockSpec((1, keep, D_MODEL), lambda b, s: (b, 0, 0))],
        out_shape=[jax.ShapeDtypeStruct((batch * seq, D_MODEL), F32),
                   jax.ShapeDtypeStruct((batch, keep, D_MODEL), F32)],
        scratch_shapes=[pltpu.VMEM((HALO + tm, D_MODEL), F32),
                        pltpu.VMEM((SUBLANES - 1, HALO + tm - SUBLANES, D_MODEL), F32),
                        pltpu.VMEM((tm, D_MODEL), F32)],
        compiler_params=pltpu.CompilerParams(dimension_semantics=("arbitrary", "arbitrary"),
                                             vmem_limit_bytes=VMEM_LIMIT),
        name="conv_ffn",
    )(h, *weights)


def _conv_ffn_sample(h, weights, state_t, layer):
    n = h.shape[0]
    keep = CONV_WIDTH - 1
    return pl.pallas_call(
        _conv_ffn_sample_kernel,
        grid=(1,),
        in_specs=[_const_spec((n, D_MODEL))] + _conv_weight_specs()
        + [pl.BlockSpec((None, keep, n, D_MODEL), lambda i: (layer, 0, 0, 0))],
        out_specs=[_const_spec((n, D_MODEL)), _const_spec((keep, n, D_MODEL))],
        out_shape=[jax.ShapeDtypeStruct((n, D_MODEL), F32), jax.ShapeDtypeStruct((keep, n, D_MODEL), F32)],
        compiler_params=pltpu.CompilerParams(dimension_semantics=("arbitrary",), vmem_limit_bytes=VMEM_LIMIT),
        name="conv_ffn_sample",
    )(h, *weights, state_t)


def _rope_tables(pos):
    inv = ROPE_THETA ** (-jnp.arange(0, HEAD_DIM, 2, dtype=F32) / HEAD_DIM)
    ang = pos.astype(F32)[:, None] * inv[None, :]
    cos, sin = jnp.cos(ang), jnp.sin(ang)
    reps = LANES // HEAD_DIM
    return jnp.tile(jnp.concatenate([cos, cos], axis=-1), (1, reps)), jnp.tile(jnp.concatenate([-sin, sin], axis=-1), (1, reps))


def kernel(x_prompt, x_sample, cache_a_k, cache_a_v, state_c_conv, norm_mix_g, norm_ffn_g, w_ffn_up, w_ffn_down, w_in_ab, q_norm_g, k_norm_g, vb_norm_g, vb_norm_b, w_spatial, b_spatial, w_out_ab, w_c_in, w_c_dw, b_c_dw, c_norm_g, c_norm_b, w_c_out):
    batch, seq, _ = x_prompt.shape
    n_dec, dec_seq, _ = x_sample.shape
    depth = norm_mix_g.shape[0]
    n_ab, _, win_buf = cache_a_k.shape[:3]
    past_len = PAST_LEN
    assert dec_seq == 1 and seq % (BRANCHES[-1][1] * Q_BLOCK) == 0 and seq % CHUNK == 0
    tm = 256

    hp = x_prompt.reshape(batch * seq, D_MODEL)
    hs = x_sample.reshape(n_dec * dec_seq, D_MODEL)
    cos_p, sin_p = _rope_tables(jnp.arange(seq, dtype=jnp.int32))
    cos_s, sin_s = _rope_tables(jnp.full((n_dec,), past_len, dtype=jnp.int32))
    head_id = jnp.arange(LANES) // HEAD_DIM
    head_mean = jnp.where(head_id[:, None] == head_id[None, :], 1.0 / HEAD_DIM, 0.0).astype(BF16)
    cache_kt = cache_a_k.transpose(0, 1, 3, 4, 2)
    cache_vt = cache_a_v.transpose(0, 1, 3, 4, 2)
    state_t = state_c_conv.transpose(0, 2, 1, 3)
    group_dim = B_WIDTH // N_GROUPS
    row2 = lambda a: a.reshape(1, -1)

    ak_p, av_p, ak_s, av_s, bv_p, bv_s, cc_p, cc_s = [], [], [], [], [], [], [], []
    for layer in range(depth):
        j = layer // 2
        gffn = row2(norm_ffn_g[layer])
        wup = w_ffn_up[layer].astype(BF16)
        wdn = w_ffn_down[layer].astype(BF16)
        gmix = row2(norm_mix_g[layer])
        if layer % 2 == 0:
            w_in = w_in_ab[j].astype(BF16)
            wout = w_out_ab[j].astype(BF16)
            qg = row2(jnp.tile(q_norm_g[j], N_HEADS))
            kg = row2(jnp.tile(k_norm_g[j], N_HEADS))
            vbg, vbb = row2(vb_norm_g[j]), row2(vb_norm_b[j])
            qp, kp, vp, up, vbp = _ab_in(hp, gmix, w_in, cos_p, sin_p, qg, kg, head_mean, vbg, vbb, tm)
            qs, ks, vs, us, vbs = _ab_in(hs, gmix, w_in, cos_s, sin_s, qg, kg, head_mean, vbg, vbb, n_dec)
            shp = (batch, seq, A_WIDTH)
            att_p = _attn_prompt(qp.reshape(shp), kp.reshape(shp), vp.reshape(shp)).reshape(batch * seq, A_WIDTH)
            att_s = _attn_sample(qs, ks, vs, cache_kt, cache_vt, j)
            w_pairs = w_spatial[j].reshape(N_GROUPS // 2, 2, CHUNK, CHUNK).transpose(0, 2, 1, 3).reshape(N_GROUPS // 2, CHUNK, 2 * CHUNK)
            b_rows = jnp.repeat(b_spatial[j].T, group_dim, axis=1)
            w0 = row2(jnp.repeat(w_spatial[j][:, 0, 0], group_dim))
            b0 = row2(jnp.repeat(b_spatial[j][:, 0], group_dim))
            hp = _mix_ffn(hp, att_p, up, vbp, w_pairs, b_rows, wout, gffn, wup, wdn, tm, False)
            hs = _mix_ffn(hs, att_s, us, vbs, w0, b0, wout, gffn, wup, wdn, n_dec, True)
            prompt_buf = min(BRANCHES[-1][0], seq)
            ak_p.append(kp.reshape(batch, seq, N_HEADS, HEAD_DIM)[:, seq - prompt_buf:])
            av_p.append(vp.reshape(batch, seq, N_HEADS, HEAD_DIM)[:, seq - prompt_buf:])
            ak_s.append(ks.reshape(n_dec, dec_seq, N_HEADS, HEAD_DIM))
            av_s.append(vs.reshape(n_dec, dec_seq, N_HEADS, HEAD_DIM))
            last_chunk_start = ((seq - 1) // CHUNK) * CHUNK
            bv_p.append(vbp.reshape(batch, seq, B_WIDTH)[:, last_chunk_start:])
            bv_s.append(vbs.reshape(n_dec, dec_seq, B_WIDTH))
        else:
            weights = (gmix, w_c_in[j].astype(BF16), w_c_dw[j], row2(b_c_dw[j]), row2(c_norm_g[j]),
                       row2(c_norm_b[j]), w_c_out[j].astype(BF16), gffn, wup, wdn)
            hp, new_cp = _conv_ffn_prompt(hp, weights, batch, seq, tm)
            hs, new_cs = _conv_ffn_sample(hs, weights, state_t, j)
            cc_p.append(new_cp)
            cc_s.append(new_cs)

    return (hp.reshape(batch, seq, D_MODEL), hs.reshape(n_dec, dec_seq, D_MODEL),
            jnp.stack(ak_p), jnp.stack(av_p), jnp.stack(ak_s), jnp.stack(av_s),
            jnp.stack(bv_p), jnp.stack(bv_s), jnp.stack(cc_p), jnp.stack(cc_s).transpose(0, 2, 1, 3))
```

```python
import functools

import jax
import jax.numpy as jnp
from jax import lax
from jax.experimental import pallas as pl
from jax.experimental.pallas import tpu as pltpu

F32 = jnp.float32
BF16 = jnp.bfloat16

D_MODEL = 1024
N_HEADS = 8
HEAD_DIM = 64
A_WIDTH = N_HEADS * HEAD_DIM
B_WIDTH = 512
N_GROUPS = 8
CHUNK = 128
Q_BLOCK = 128
BRANCHES = ((128, 1), (512, 4), (2048, 16))
ROPE_THETA = 10000.0
CONV_WIDTH = 31
PAST_LEN = 8192
D_FF = 4 * D_MODEL
EPS = 1e-6
IN_AB = 3 * A_WIDTH + 2 * B_WIDTH

LANES = 128
SUBLANES = 8
HALO = 32
CONV_ROWS = 64
FF_CHUNK = 1024
NEG = -1e30
LOG2_E = 1.4426950408889634
QK_SCALE = HEAD_DIM ** -0.5 * LOG2_E
ATTN_GROUP = 4
VMEM_LIMIT = 56 * 1024 * 1024


def _rmsnorm(x, g):
    ms = jnp.mean(x * x, axis=-1, keepdims=True)
    return x * lax.rsqrt(ms + EPS) * g


def _layernorm(x, g, b):
    mu = jnp.mean(x, axis=-1, keepdims=True)
    xc = x - mu
    var = jnp.mean(xc * xc, axis=-1, keepdims=True)
    return xc * lax.rsqrt(var + EPS) * g + b


def _dot(a, b):
    return jnp.dot(a, b, preferred_element_type=F32)


def _ffn(h1, g, wup_ref, wdn_ref):
    n = _rmsnorm(h1, g).astype(BF16)
    acc = h1
    for c in range(D_FF // FF_CHUNK):
        a = _dot(n, wup_ref[:, c * FF_CHUNK:(c + 1) * FF_CHUNK])
        a = jnp.maximum(a, 0.0)
        a = (a * a).astype(BF16)
        acc = acc + _dot(a, wdn_ref[c * FF_CHUNK:(c + 1) * FF_CHUNK, :])
    return acc


def _ab_in_kernel(h_ref, g_ref, w_ref, cos_ref, sin_ref, qg_ref, kg_ref, hm_ref, vbg_ref, vbb_ref, *refs,
                  n_prev, transposed):
    prev_refs, refs = refs[:2 if n_prev else 0], refs[2 if n_prev else 0:]
    q_ref, k_ref, v_ref, u_ref, vb_ref = refs[:5]
    kt_ref, vt_ref = refs[5:] if transposed else (None, None)
    if n_prev:
        kt_ref[0:n_prev] = prev_refs[0][...]
        vt_ref[0:n_prev] = prev_refs[1][...]
    n = _rmsnorm(h_ref[...], g_ref[...]).astype(BF16)
    cos = cos_ref[...]
    sin = sin_ref[...]
    hm = hm_ref[...]
    lane = lax.broadcasted_iota(jnp.int32, (1, LANES), 1)
    first_half = (lane & (HEAD_DIM // 2)) == 0

    def head_norm_rope(z, gain_ref, out_ref, scale, t_ref):
        for c in range(A_WIDTH // LANES):
            sl = slice(c * LANES, (c + 1) * LANES)
            zc = z[:, sl]
            sq = zc * zc
            hi = sq.astype(BF16)
            lo = (sq - hi.astype(F32)).astype(BF16)
            ms = _dot(hi, hm) + _dot(lo, hm)
            y = zc * lax.rsqrt(ms + EPS) * gain_ref[:, sl]
            partner = jnp.where(first_half, pltpu.roll(y, LANES - HEAD_DIM // 2, 1),
                                pltpu.roll(y, HEAD_DIM // 2, 1))
            out = y * cos + partner * sin
            if scale != 1.0:
                out = out * scale
            out_ref[:, sl] = out
            if t_ref is not None:
                t_ref[n_prev, sl, :] = out.T

    head_norm_rope(_dot(n, w_ref[:, 0:A_WIDTH]), qg_ref, q_ref, QK_SCALE, None)
    head_norm_rope(_dot(n, w_ref[:, A_WIDTH:2 * A_WIDTH]), kg_ref, k_ref, 1.0, kt_ref)
    v = _dot(n, w_ref[:, 2 * A_WIDTH:3 * A_WIDTH])
    v_ref[...] = v
    if transposed:
        for c in range(A_WIDTH // LANES):
            sl = slice(c * LANES, (c + 1) * LANES)
            vt_ref[n_prev, sl, :] = v[:, sl].T
    u_ref[...] = jax.nn.gelu(_dot(n, w_ref[:, 3 * A_WIDTH:3 * A_WIDTH + B_WIDTH]))
    vb = jax.nn.gelu(_dot(n, w_ref[:, 3 * A_WIDTH + B_WIDTH:IN_AB]))
    vb_ref[...] = _layernorm(vb, vbg_ref[...], vbb_ref[...])


def _const_spec(shape):
    nd = len(shape)
    return pl.BlockSpec(shape, lambda *_: (0,) * nd, pipeline_mode=pl.Buffered(1))


def _ab_in(h, g, w_bf, cos_t, sin_t, qg, kg, hm, vbg, vbb, tm, seq=None, prev_t=()):
    t = h.shape[0]
    n_pos_blocks = cos_t.shape[0] // tm
    row = lambda i: (i, 0)
    tab = lambda i: (i % n_pos_blocks, 0)
    out = jax.ShapeDtypeStruct((t, A_WIDTH), F32)
    in_specs = [
        pl.BlockSpec((tm, D_MODEL), row), _const_spec((1, D_MODEL)), _const_spec((D_MODEL, IN_AB)),
        pl.BlockSpec((tm, LANES), tab), pl.BlockSpec((tm, LANES), tab),
        _const_spec((1, A_WIDTH)), _const_spec((1, A_WIDTH)), _const_spec((LANES, LANES)),
        _const_spec((1, B_WIDTH)), _const_spec((1, B_WIDTH)),
    ]
    out_specs = [pl.BlockSpec((tm, A_WIDTH), row)] * 5
    out_shape = [out] * 5
    n_prev = prev_t[0].shape[0] if prev_t else 0
    if seq is not None:
        tiles_per_seq = seq // tm
        t_map = lambda i: (0, i // tiles_per_seq, 0, i % tiles_per_seq)
        if prev_t:
            in_specs += [pl.BlockSpec((n_prev, None, A_WIDTH, tm), t_map)] * 2
        out_specs += [pl.BlockSpec((n_prev + 1, None, A_WIDTH, tm), t_map)] * 2
        out_shape += [jax.ShapeDtypeStruct((n_prev + 1, t // seq, A_WIDTH, seq), F32)] * 2
    return pl.pallas_call(
        functools.partial(_ab_in_kernel, n_prev=n_prev, transposed=seq is not None),
        grid=(t // tm,),
        in_specs=in_specs,
        out_specs=out_specs,
        out_shape=out_shape,
        compiler_params=pltpu.CompilerParams(dimension_semantics=("arbitrary",), vmem_limit_bytes=VMEM_LIMIT),
        name="ab_in",
    )(h, g, w_bf, cos_t, sin_t, qg, kg, hm, vbg, vbb, *prev_t)


def _attn_prompt_kernel(q_ref, k_ref, v_ref, o_ref, ob_scr, lse_scr, *, seq):
    lane = lax.broadcasted_iota(jnp.int32, (1, LANES), 1)
    head0 = lane < HEAD_DIM
    ri = lax.broadcasted_iota(jnp.int32, (2 * Q_BLOCK, 2 * Q_BLOCK), 0) & (Q_BLOCK - 1)
    ci = lax.broadcasted_iota(jnp.int32, (2 * Q_BLOCK, 2 * Q_BLOCK), 1)
    band_two = jnp.where(ci < Q_BLOCK, ci - ri, ri + Q_BLOCK - ci) >= 0
    band_one = (lax.broadcasted_iota(jnp.int32, (2 * Q_BLOCK, Q_BLOCK), 1)
                <= lax.broadcasted_iota(jnp.int32, (2 * Q_BLOCK, Q_BLOCK), 0) & (Q_BLOCK - 1))

    def rows(start, dil):
        if dil == 1:
            return pl.ds(start if isinstance(start, int) else pl.multiple_of(start, Q_BLOCK), Q_BLOCK)
        return pl.ds(start, Q_BLOCK, stride=dil)

    def load(dil, start, with_prev):
        cur = rows(start, dil)
        q = q_ref[0, cur, :]
        q2 = jnp.concatenate([jnp.where(head0, q, 0.0), jnp.where(head0, 0.0, q)], axis=0).astype(BF16)
        k = k_ref[0, cur, :].astype(BF16)
        v = v_ref[0, cur, :].astype(BF16)
        if with_prev:
            prev = rows(start - dil * Q_BLOCK, dil)
            k = jnp.concatenate([k_ref[0, prev, :].astype(BF16), k], axis=0)
            v = jnp.concatenate([v_ref[0, prev, :].astype(BF16), v], axis=0)
        return q2, k, v

    def attend(q2, k, v, with_prev):
        n_keys = k.shape[0]
        s = lax.dot_general(q2, k, (((1,), (1,)), ((), ())), preferred_element_type=F32)
        s = jnp.where(band_two if with_prev else band_one, s, NEG)
        m = jnp.max(s, axis=-1, keepdims=True)
        p = jnp.exp2(s - m).astype(BF16)
        r = _dot(p, jnp.concatenate([v, jnp.ones((n_keys, LANES), BF16)], axis=1))
        den = r[:, LANES:]
        o = r[:, :LANES] * (1.0 / den)
        lse = m + jnp.log2(den)
        return jnp.where(head0, o[:Q_BLOCK], o[Q_BLOCK:]), jnp.where(head0, lse[:Q_BLOCK], lse[Q_BLOCK:])

    def blocks(branch, dil, starts, with_prev):
        loaded = [load(dil, st, with_prev) for st in starts]
        results = [attend(*ld, with_prev) for ld in loaded]
        for st, (o, lse) in zip(starts, results):
            ob_scr[branch, rows(st, dil), :] = o
            lse_scr[branch, rows(st, dil), :] = lse

    def grouped_loop(total, body):
        n_full = total // ATTN_GROUP

        def loop_body(it, carry):
            body([it * ATTN_GROUP + g for g in range(ATTN_GROUP)])
            return carry

        if n_full:
            lax.fori_loop(0, n_full, loop_body, 0)
        if total % ATTN_GROUP:
            body(list(range(n_full * ATTN_GROUP, total)))

    for branch, (window, dil) in enumerate(BRANCHES):
        assert window // dil == Q_BLOCK
        sub_len = seq // dil
        n_blocks = sub_len // Q_BLOCK

        def first_body(items, branch=branch, dil=dil):
            blocks(branch, dil, items, False)

        grouped_loop(dil, first_body)
        if n_blocks > 1:
            def rest_body(items, branch=branch, dil=dil, n_blocks=n_blocks):
                starts = [it // (n_blocks - 1) + dil * Q_BLOCK * (it % (n_blocks - 1) + 1) for it in items]
                blocks(branch, dil, starts, True)

            grouped_loop(dil * (n_blocks - 1), rest_body)

    rows_per_step = 256

    def merge_body(it, carry):
        sl = pl.ds(pl.multiple_of(it * rows_per_step, rows_per_step), rows_per_step)
        lses = [lse_scr[b, sl, :] for b in range(len(BRANCHES))]
        m = functools.reduce(jnp.maximum, lses)
        ws = [jnp.exp2(l - m) for l in lses]
        num = sum(w * ob_scr[b, sl, :] for b, w in enumerate(ws))
        o_ref[0, sl, :] = (num * (1.0 / sum(ws))).astype(o_ref.dtype)
        return carry

    lax.fori_loop(0, seq // rows_per_step, merge_body, 0)


def _attn_prompt(q, k, v):
    b, seq, _ = q.shape
    spec = pl.BlockSpec((1, seq, LANES), lambda i, j: (i, 0, j))
    return pl.pallas_call(
        functools.partial(_attn_prompt_kernel, seq=seq),
        grid=(b, A_WIDTH // LANES),
        in_specs=[spec, spec, spec],
        out_specs=spec,
        out_shape=jax.ShapeDtypeStruct((b, seq, A_WIDTH), BF16),
        scratch_shapes=[pltpu.VMEM((len(BRANCHES), seq, LANES), F32),
                        pltpu.VMEM((len(BRANCHES), seq, LANES), F32)],
        compiler_params=pltpu.CompilerParams(dimension_semantics=("arbitrary", "arbitrary"),
                                             vmem_limit_bytes=VMEM_LIMIT),
        name="attn_prompt",
    )(q, k, v)


def _attn_sample_kernel(q_ref, kn_ref, vn_ref, kt_ref, vt_ref, o_ref, *, win_buf):
    back = win_buf - lax.broadcasted_iota(jnp.int32, (1, win_buf), 1)
    masks = [jnp.where((back & (dil - 1)) == 0, back, window + 1) <= window for window, dil in BRANCHES]
    for h in range(N_HEADS):
        q = q_ref[h]
        k_new = kn_ref[h]
        v_new = vn_ref[h]
        s = jnp.sum(kt_ref[h] * q, axis=0, keepdims=True)
        s_new = jnp.sum(q * k_new, axis=0, keepdims=True)
        ps, pns, lses = [], [], []
        for mask in masks:
            sb = jnp.where(mask, s, NEG)
            m = jnp.maximum(jnp.max(sb, axis=1, keepdims=True), s_new)
            p = jnp.exp2(sb - m)
            pn = jnp.exp2(s_new - m)
            den = jnp.sum(p, axis=1, keepdims=True) + pn
            inv = 1.0 / den
            ps.append(p * inv)
            pns.append(pn * inv)
            lses.append(m + jnp.log2(den))
        mm = functools.reduce(jnp.maximum, lses)
        ws = [jnp.exp2(l - mm) for l in lses]
        inv = 1.0 / sum(ws)
        p_all = sum(w * p for w, p in zip(ws, ps)) * inv
        pn_all = sum(w * pn for w, pn in zip(ws, pns)) * inv
        o_ref[h] = jnp.sum(vt_ref[h] * p_all, axis=1, keepdims=True) + pn_all * v_new


def _attn_sample(q, k_new, v_new, cache_kt, cache_vt, layer):
    n, win_buf = cache_kt.shape[1], cache_kt.shape[-1]
    for window, dil in BRANCHES:
        assert window <= win_buf <= PAST_LEN and dil & (dil - 1) == 0
    col = pl.BlockSpec((None, N_HEADS, HEAD_DIM, 1), lambda i: (i, 0, 0, 0))
    cache = pl.BlockSpec((None, None, N_HEADS, HEAD_DIM, win_buf), lambda i: (layer, i, 0, 0, 0))
    as_cols = lambda a: a.reshape(n, N_HEADS, HEAD_DIM, 1)
    return pl.pallas_call(
        functools.partial(_attn_sample_kernel, win_buf=win_buf),
        grid=(n,),
        in_specs=[col, col, col, cache, cache],
        out_specs=col,
        out_shape=jax.ShapeDtypeStruct((n, N_HEADS, HEAD_DIM, 1), F32),
        compiler_params=pltpu.CompilerParams(dimension_semantics=("arbitrary",), vmem_limit_bytes=VMEM_LIMIT),
        name="attn_sample",
    )(as_cols(q), as_cols(k_new), as_cols(v_new), cache_kt, cache_vt).reshape(n, A_WIDTH)


def _mix_ffn_prompt_kernel(h_ref, att_ref, u_ref, vb_ref, wsp_ref, bsp_ref, wout_ref, gffn_ref, wup_ref, wdn_ref,
                           o_ref, gate_scr, *, tm):
    lane = lax.broadcasted_iota(jnp.int32, (1, LANES), 1)
    lo = lane < B_WIDTH // N_GROUPS
    ri = lax.broadcasted_iota(jnp.int32, (CHUNK, 2 * CHUNK), 0)
    ci = lax.broadcasted_iota(jnp.int32, (CHUNK, 2 * CHUNK), 1)
    causal = (ci & (CHUNK - 1)) <= ri
    n_pairs = B_WIDTH // LANES
    w_pairs = [jnp.where(causal, wsp_ref[p], 0.0).astype(BF16) for p in range(n_pairs)]
    for c in range(tm // CHUNK):
        rows = slice(c * CHUNK, (c + 1) * CHUNK)
        for p in range(n_pairs):
            sl = slice(p * LANES, (p + 1) * LANES)
            slab = vb_ref[rows, sl]
            rhs = jnp.concatenate([jnp.where(lo, slab, 0.0), jnp.where(lo, 0.0, slab)], axis=0).astype(BF16)
            mixed = _dot(w_pairs[p], rhs) + bsp_ref[:, sl]
            gate_scr[rows, sl] = (u_ref[rows, sl] * mixed).astype(BF16)
    mix = _dot(att_ref[...], wout_ref[0:A_WIDTH, :]) + _dot(gate_scr[...], wout_ref[A_WIDTH:, :])
    h1 = h_ref[...] + mix
    o_ref[...] = _ffn(h1, gffn_ref[...], wup_ref, wdn_ref)


def _mix_ffn_sample_kernel(h_ref, att_ref, u_ref, vb_ref, w0_ref, b0_ref, wout_ref, gffn_ref, wup_ref, wdn_ref,
                           o_ref):
    gate = (u_ref[...] * (vb_ref[...] * w0_ref[...] + b0_ref[...])).astype(BF16)
    mix = _dot(att_ref[...].astype(BF16), wout_ref[0:A_WIDTH, :]) + _dot(gate, wout_ref[A_WIDTH:, :])
    h1 = h_ref[...] + mix
    o_ref[...] = _ffn(h1, gffn_ref[...], wup_ref, wdn_ref)


def _mix_ffn(h, att, u, vb, gate_w, gate_b, wout_bf, gffn, wup_bf, wdn_bf, tm, sample):
    t = h.shape[0]
    row = lambda i: (i, 0)
    if sample:
        kern = _mix_ffn_sample_kernel
        scratch = []
    else:
        kern = functools.partial(_mix_ffn_prompt_kernel, tm=tm)
        scratch = [pltpu.VMEM((tm, B_WIDTH), BF16)]
    return pl.pallas_call(
        kern,
        grid=(t // tm,),
        in_specs=[
            pl.BlockSpec((tm, D_MODEL), row), pl.BlockSpec((tm, A_WIDTH), row),
            pl.BlockSpec((tm, B_WIDTH), row), pl.BlockSpec((tm, B_WIDTH), row),
            _const_spec(gate_w.shape), _const_spec(gate_b.shape),
            _const_spec((A_WIDTH + B_WIDTH, D_MODEL)), _const_spec((1, D_MODEL)),
            _const_spec((D_MODEL, D_FF)), _const_spec((D_FF, D_MODEL)),
        ],
        out_specs=pl.BlockSpec((tm, D_MODEL), row),
        out_shape=jax.ShapeDtypeStruct((t, D_MODEL), F32),
        scratch_shapes=scratch,
        compiler_params=pltpu.CompilerParams(dimension_semantics=("arbitrary",), vmem_limit_bytes=VMEM_LIMIT),
        name="mix_ffn_sample" if sample else "mix_ffn",
    )(h, att, u, vb, gate_w, gate_b, wout_bf, gffn, wup_bf, wdn_bf)


def _glu_in(h, gmix, win_ref):
    n = _rmsnorm(h, gmix).astype(BF16)
    a = _dot(n, win_ref[:, 0:D_MODEL])
    gate = _dot(n, win_ref[:, D_MODEL:])
    return a * jax.nn.sigmoid(gate)


def _conv_tail(h, y, cg_ref, cb_ref, wout_ref, gffn_ref, wup_ref, wdn_ref):
    z = _layernorm(y, cg_ref[...], cb_ref[...])
    z = (z * jax.nn.sigmoid(z)).astype(BF16)
    h1 = h + _dot(z, wout_ref[...])
    return _ffn(h1, gffn_ref[...], wup_ref, wdn_ref)


def _conv_ffn_prompt_kernel(h_ref, gmix_ref, win_ref, wdw_ref, bdw_ref, cg_ref, cb_ref, wout_ref, gffn_ref,
                            wup_ref, wdn_ref, o_ref, cc_ref, xbuf, xs_scr, y_scr, *, tm):
    s = pl.program_id(1)
    keep = CONV_WIDTH - 1

    @pl.when(s == 0)
    def _():
        xbuf[0:HALO, :] = jnp.zeros((HALO, D_MODEL), F32)

    h = h_ref[...]
    xbuf[HALO:HALO + tm, :] = _glu_in(h, gmix_ref[...], win_ref)

    shift_rows = tm + HALO - SUBLANES
    for b in range(1, SUBLANES):
        xs_scr[b - 1] = xbuf[b:b + shift_rows, :]

    for col in range(D_MODEL // LANES):
        cs = slice(col * LANES, (col + 1) * LANES)
        taps = [wdw_ref[w:w + 1, cs] for w in range(CONV_WIDTH)]
        bias = jnp.broadcast_to(bdw_ref[:, cs], (CONV_ROWS, LANES))
        for r0 in range(0, tm, CONV_ROWS):
            acc = bias
            for w in range(CONV_WIDTH):
                a, b = divmod(HALO - keep + w, SUBLANES)
                rows = slice(r0 + SUBLANES * a, r0 + SUBLANES * a + CONV_ROWS)
                x = xbuf[rows, cs] if b == 0 else xs_scr[b - 1, rows, cs]
                acc = acc + x * taps[w]
            y_scr[r0:r0 + CONV_ROWS, cs] = acc

    cc_ref[0] = xbuf[HALO + tm - keep:HALO + tm, :]
    xbuf[0:HALO, :] = xbuf[tm:tm + HALO, :]
    o_ref[...] = _conv_tail(h, y_scr[...], cg_ref, cb_ref, wout_ref, gffn_ref, wup_ref, wdn_ref)


def _conv_ffn_sample_kernel(h_ref, gmix_ref, win_ref, wdw_ref, bdw_ref, cg_ref, cb_ref, wout_ref, gffn_ref,
                            wup_ref, wdn_ref, st_ref, o_ref, cc_ref):
    keep = CONV_WIDTH - 1
    h = h_ref[...]
    x = _glu_in(h, gmix_ref[...], win_ref)
    y = x * wdw_ref[keep:keep + 1, :] + bdw_ref[...]
    for w in range(keep):
        y = y + st_ref[w] * wdw_ref[w:w + 1, :]
    for w in range(keep - 1):
        cc_ref[w] = st_ref[w + 1]
    cc_ref[keep - 1] = x
    o_ref[...] = _conv_tail(h, y, cg_ref, cb_ref, wout_ref, gffn_ref, wup_ref, wdn_ref)


def _conv_weight_specs():
    return [
        _const_spec((1, D_MODEL)), _const_spec((D_MODEL, 2 * D_MODEL)), _const_spec((CONV_WIDTH, D_MODEL)),
        _const_spec((1, D_MODEL)), _const_spec((1, D_MODEL)), _const_spec((1, D_MODEL)),
        _const_spec((D_MODEL, D_MODEL)), _const_spec((1, D_MODEL)),
        _const_spec((D_MODEL, D_FF)), _const_spec((D_FF, D_MODEL)),
    ]


def _conv_ffn_prompt(h, weights, batch, seq, tm):
    keep = CONV_WIDTH - 1
    n_s = seq // tm
    row = lambda b, s: (b * n_s + s, 0)
    return pl.pallas_call(
        functools.partial(_conv_ffn_prompt_kernel, tm=tm),
        grid=(batch, n_s),
        in_specs=[pl.BlockSpec((tm, D_MODEL), row)] + _conv_weight_specs(),
        out_specs=[pl.BlockSpec((tm, D_MODEL), row), pl.BlockSpec((1, keep, D_MODEL), lambda b, s: (b, 0, 0))],
        out_shape=[jax.ShapeDtypeStruct((batch * seq, D_MODEL), F32),
                   jax.ShapeDtypeStruct((batch, keep, D_MODEL), F32)],
        scratch_shapes=[pltpu.VMEM((HALO + tm, D_MODEL), F32),
                        pltpu.VMEM((SUBLANES - 1, HALO + tm - SUBLANES, D_MODEL), F32),
                        pltpu.VMEM((tm, D_MODEL), F32)],
        compiler_params=pltpu.CompilerParams(dimension_semantics=("arbitrary", "arbitrary"),
                                             vmem_limit_bytes=VMEM_LIMIT),
        name="conv_ffn",
    )(h, *weights)


def _conv_ffn_sample(h, weights, state_t, layer):
    n = h.shape[0]
    keep = CONV_WIDTH - 1
    return pl.pallas_call(
        _conv_ffn_sample_kernel,
        grid=(1,),
        in_specs=[_const_spec((n, D_MODEL))] + _conv_weight_specs()
        + [pl.BlockSpec((None, keep, n, D_MODEL), lambda i: (layer, 0, 0, 0))],
        out_specs=[_const_spec((n, D_MODEL)), _const_spec((keep, n, D_MODEL))],
        out_shape=[jax.ShapeDtypeStruct((n, D_MODEL), F32), jax.ShapeDtypeStruct((keep, n, D_MODEL), F32)],
        compiler_params=pltpu.CompilerParams(dimension_semantics=("arbitrary",), vmem_limit_bytes=VMEM_LIMIT),
        name="conv_ffn_sample",
    )(h, *weights, state_t)


def _rope_tables(pos):
    inv = ROPE_THETA ** (-jnp.arange(0, HEAD_DIM, 2, dtype=F32) / HEAD_DIM)
    ang = pos.astype(F32)[:, None] * inv[None, :]
    cos, sin = jnp.cos(ang), jnp.sin(ang)
    reps = LANES // HEAD_DIM
    return jnp.tile(jnp.concatenate([cos, cos], axis=-1), (1, reps)), jnp.tile(jnp.concatenate([-sin, sin], axis=-1), (1, reps))


def kernel(x_prompt, x_sample, cache_a_k, cache_a_v, state_c_conv, norm_mix_g, norm_ffn_g, w_ffn_up, w_ffn_down, w_in_ab, q_norm_g, k_norm_g, vb_norm_g, vb_norm_b, w_spatial, b_spatial, w_out_ab, w_c_in, w_c_dw, b_c_dw, c_norm_g, c_norm_b, w_c_out):
    batch, seq, _ = x_prompt.shape
    n_dec, dec_seq, _ = x_sample.shape
    depth = norm_mix_g.shape[0]
    n_ab, _, win_buf = cache_a_k.shape[:3]
    past_len = PAST_LEN
    assert dec_seq == 1 and seq % (BRANCHES[-1][1] * Q_BLOCK) == 0 and seq % CHUNK == 0
    tm = 256

    hp = x_prompt.reshape(batch * seq, D_MODEL)
    hs = x_sample.reshape(n_dec * dec_seq, D_MODEL)
    cos_p, sin_p = _rope_tables(jnp.arange(seq, dtype=jnp.int32))
    cos_s, sin_s = _rope_tables(jnp.full((n_dec,), past_len, dtype=jnp.int32))
    head_id = jnp.arange(LANES) // HEAD_DIM
    head_mean = jnp.where(head_id[:, None] == head_id[None, :], 1.0 / HEAD_DIM, 0.0).astype(BF16)
    cache_kt = cache_a_k.transpose(0, 1, 3, 4, 2)
    cache_vt = cache_a_v.transpose(0, 1, 3, 4, 2)
    state_t = state_c_conv.transpose(0, 2, 1, 3)
    group_dim = B_WIDTH // N_GROUPS
    row2 = lambda a: a.reshape(1, -1)

    ak_s, av_s, bv_p, bv_s, cc_p, cc_s = [], [], [], [], [], []
    kv_t = ()
    for layer in range(depth):
        j = layer // 2
        gffn = row2(norm_ffn_g[layer])
        wup = w_ffn_up[layer].astype(BF16)
        wdn = w_ffn_down[layer].astype(BF16)
        gmix = row2(norm_mix_g[layer])
        if layer % 2 == 0:
            w_in = w_in_ab[j].astype(BF16)
            wout = w_out_ab[j].astype(BF16)
            qg = row2(jnp.tile(q_norm_g[j], N_HEADS))
            kg = row2(jnp.tile(k_norm_g[j], N_HEADS))
            vbg, vbb = row2(vb_norm_g[j]), row2(vb_norm_b[j])
            qp, kp, vp, up, vbp, *kv_t = _ab_in(hp, gmix, w_in, cos_p, sin_p, qg, kg, head_mean, vbg, vbb, tm,
                                                seq=seq, prev_t=tuple(kv_t))
            qs, ks, vs, us, vbs = _ab_in(hs, gmix, w_in, cos_s, sin_s, qg, kg, head_mean, vbg, vbb, n_dec)
            shp = (batch, seq, A_WIDTH)
            att_p = _attn_prompt(qp.reshape(shp), kp.reshape(shp), vp.reshape(shp)).reshape(batch * seq, A_WIDTH)
            att_s = _attn_sample(qs, ks, vs, cache_kt, cache_vt, j)
            w_pairs = w_spatial[j].reshape(N_GROUPS // 2, 2, CHUNK, CHUNK).transpose(0, 2, 1, 3).reshape(N_GROUPS // 2, CHUNK, 2 * CHUNK)
            b_rows = jnp.repeat(b_spatial[j].T, group_dim, axis=1)
            w0 = row2(jnp.repeat(w_spatial[j][:, 0, 0], group_dim))
            b0 = row2(jnp.repeat(b_spatial[j][:, 0], group_dim))
            hp = _mix_ffn(hp, att_p, up, vbp, w_pairs, b_rows, wout, gffn, wup, wdn, tm, False)
            hs = _mix_ffn(hs, att_s, us, vbs, w0, b0, wout, gffn, wup, wdn, n_dec, True)
            ak_s.append(ks.reshape(n_dec, dec_seq, N_HEADS, HEAD_DIM))
            av_s.append(vs.reshape(n_dec, dec_seq, N_HEADS, HEAD_DIM))
            last_chunk_start = ((seq - 1) // CHUNK) * CHUNK
            bv_p.append(vbp.reshape(batch, seq, B_WIDTH)[:, last_chunk_start:])
            bv_s.append(vbs.reshape(n_dec, dec_seq, B_WIDTH))
        else:
            weights = (gmix, w_c_in[j].astype(BF16), w_c_dw[j], row2(b_c_dw[j]), row2(c_norm_g[j]),
                       row2(c_norm_b[j]), w_c_out[j].astype(BF16), gffn, wup, wdn)
            hp, new_cp = _conv_ffn_prompt(hp, weights, batch, seq, tm)
            hs, new_cs = _conv_ffn_sample(hs, weights, state_t, j)
            cc_p.append(new_cp)
            cc_s.append(new_cs)

    prompt_buf = min(BRANCHES[-1][0], seq)
    ak_p, av_p = [a.reshape(n_ab, batch, N_HEADS, HEAD_DIM, seq).transpose(0, 1, 4, 2, 3)[:, :, seq - prompt_buf:]
                  for a in kv_t]
    return (hp.reshape(batch, seq, D_MODEL), hs.reshape(n_dec, dec_seq, D_MODEL),
            ak_p, av_p, jnp.stack(ak_s), jnp.stack(av_s),
            jnp.stack(bv_p), jnp.stack(bv_s), jnp.stack(cc_p), jnp.stack(cc_s).transpose(0, 2, 1, 3))
```

```python
import functools

import jax
import jax.numpy as jnp
from jax import lax
from jax.experimental import pallas as pl
from jax.experimental.pallas import tpu as pltpu

F32 = jnp.float32
BF16 = jnp.bfloat16

D_MODEL = 1024
N_HEADS = 8
HEAD_DIM = 64
A_WIDTH = N_HEADS * HEAD_DIM
B_WIDTH = 512
N_GROUPS = 8
CHUNK = 128
Q_BLOCK = 128
BRANCHES = ((128, 1), (512, 4), (2048, 16))
ROPE_THETA = 10000.0
CONV_WIDTH = 31
PAST_LEN = 8192
D_FF = 4 * D_MODEL
EPS = 1e-6
IN_AB = 3 * A_WIDTH + 2 * B_WIDTH

LANES = 128
SUBLANES = 8
HALO = 32
CONV_ROWS = 64
FF_CHUNK = 1024
NEG = -1e30
LOG2_E = 1.4426950408889634
QK_SCALE = HEAD_DIM ** -0.5 * LOG2_E
ATTN_GROUP = 4
VMEM_LIMIT = 56 * 1024 * 1024


def _rmsnorm(x, g):
    ms = jnp.mean(x * x, axis=-1, keepdims=True)
    return x * lax.rsqrt(ms + EPS) * g


def _layernorm(x, g, b):
    mu = jnp.mean(x, axis=-1, keepdims=True)
    xc = x - mu
    var = jnp.mean(xc * xc, axis=-1, keepdims=True)
    return xc * lax.rsqrt(var + EPS) * g + b


def _dot(a, b):
    return jnp.dot(a, b, preferred_element_type=F32)


def _ffn(h1, g, wup_ref, wdn_ref):
    n = _rmsnorm(h1, g).astype(BF16)
    acc = h1
    for c in range(D_FF // FF_CHUNK):
        a = _dot(n, wup_ref[:, c * FF_CHUNK:(c + 1) * FF_CHUNK])
        a = jnp.maximum(a, 0.0)
        a = (a * a).astype(BF16)
        acc = acc + _dot(a, wdn_ref[c * FF_CHUNK:(c + 1) * FF_CHUNK, :])
    return acc


def _ab_in_kernel(h_ref, g_ref, w_ref, cos_ref, sin_ref, qg_ref, kg_ref, hm_ref, vbg_ref, vbb_ref, *refs,
                  n_prev, transposed):
    prev_refs, refs = refs[:2 if n_prev else 0], refs[2 if n_prev else 0:]
    q_ref, k_ref, v_ref, u_ref, vb_ref = refs[:5]
    kt_ref, vt_ref = refs[5:] if transposed else (None, None)
    if n_prev:
        kt_ref[0:n_prev] = prev_refs[0][...]
        vt_ref[0:n_prev] = prev_refs[1][...]
    n = _rmsnorm(h_ref[...], g_ref[...]).astype(BF16)
    cos = cos_ref[...]
    sin = sin_ref[...]
    hm = hm_ref[...]
    lane = lax.broadcasted_iota(jnp.int32, (1, LANES), 1)
    first_half = (lane & (HEAD_DIM // 2)) == 0

    def head_norm_rope(z, gain_ref, out_ref, scale, t_ref):
        for c in range(A_WIDTH // LANES):
            sl = slice(c * LANES, (c + 1) * LANES)
            zc = z[:, sl]
            sq = zc * zc
            hi = sq.astype(BF16)
            lo = (sq - hi.astype(F32)).astype(BF16)
            ms = _dot(hi, hm) + _dot(lo, hm)
            y = zc * lax.rsqrt(ms + EPS) * gain_ref[:, sl]
            partner = jnp.where(first_half, pltpu.roll(y, LANES - HEAD_DIM // 2, 1),
                                pltpu.roll(y, HEAD_DIM // 2, 1))
            out = y * cos + partner * sin
            if scale != 1.0:
                out = out * scale
            out_ref[:, sl] = out
            if t_ref is not None:
                t_ref[n_prev, sl, :] = out.T

    head_norm_rope(_dot(n, w_ref[:, 0:A_WIDTH]), qg_ref, q_ref, QK_SCALE, None)
    head_norm_rope(_dot(n, w_ref[:, A_WIDTH:2 * A_WIDTH]), kg_ref, k_ref, 1.0, kt_ref)
    v = _dot(n, w_ref[:, 2 * A_WIDTH:3 * A_WIDTH])
    v_ref[...] = v
    if transposed:
        for c in range(A_WIDTH // LANES):
            sl = slice(c * LANES, (c + 1) * LANES)
            vt_ref[n_prev, sl, :] = v[:, sl].T
    u_ref[...] = jax.nn.gelu(_dot(n, w_ref[:, 3 * A_WIDTH:3 * A_WIDTH + B_WIDTH]))
    vb = jax.nn.gelu(_dot(n, w_ref[:, 3 * A_WIDTH + B_WIDTH:IN_AB]))
    vb_ref[...] = _layernorm(vb, vbg_ref[...], vbb_ref[...])


def _const_spec(shape):
    nd = len(shape)
    return pl.BlockSpec(shape, lambda *_: (0,) * nd, pipeline_mode=pl.Buffered(1))


def _layer_spec(shape, layer):
    nd = len(shape)
    return pl.BlockSpec((None, *shape), lambda *_: (layer,) + (0,) * nd, pipeline_mode=pl.Buffered(1))


def _ab_in(h, g, w_bf, j, cos_t, sin_t, qg, kg, hm, vbg, vbb, tm, seq=None, prev_t=()):
    t = h.shape[0]
    n_pos_blocks = cos_t.shape[0] // tm
    row = lambda i: (i, 0)
    tab = lambda i: (i % n_pos_blocks, 0)
    out = jax.ShapeDtypeStruct((t, A_WIDTH), F32)
    in_specs = [
        pl.BlockSpec((tm, D_MODEL), row), _const_spec((1, D_MODEL)), _layer_spec((D_MODEL, IN_AB), j),
        pl.BlockSpec((tm, LANES), tab), pl.BlockSpec((tm, LANES), tab),
        _const_spec((1, A_WIDTH)), _const_spec((1, A_WIDTH)), _const_spec((LANES, LANES)),
        _const_spec((1, B_WIDTH)), _const_spec((1, B_WIDTH)),
    ]
    out_specs = [pl.BlockSpec((tm, A_WIDTH), row)] * 5
    out_shape = [out] * 5
    n_prev = prev_t[0].shape[0] if prev_t else 0
    if seq is not None:
        tiles_per_seq = seq // tm
        t_map = lambda i: (0, i // tiles_per_seq, 0, i % tiles_per_seq)
        if prev_t:
            in_specs += [pl.BlockSpec((n_prev, None, A_WIDTH, tm), t_map)] * 2
        out_specs += [pl.BlockSpec((n_prev + 1, None, A_WIDTH, tm), t_map)] * 2
        out_shape += [jax.ShapeDtypeStruct((n_prev + 1, t // seq, A_WIDTH, seq), F32)] * 2
    return pl.pallas_call(
        functools.partial(_ab_in_kernel, n_prev=n_prev, transposed=seq is not None),
        grid=(t // tm,),
        in_specs=in_specs,
        out_specs=out_specs,
        out_shape=out_shape,
        compiler_params=pltpu.CompilerParams(dimension_semantics=("arbitrary",), vmem_limit_bytes=VMEM_LIMIT),
        name="ab_in",
    )(h, g, w_bf, cos_t, sin_t, qg, kg, hm, vbg, vbb, *prev_t)


def _attn_prompt_kernel(q_ref, k_ref, v_ref, o_ref, ob_scr, lse_scr, *, seq):
    lane = lax.broadcasted_iota(jnp.int32, (1, LANES), 1)
    head0 = lane < HEAD_DIM
    ri = lax.broadcasted_iota(jnp.int32, (2 * Q_BLOCK, 2 * Q_BLOCK), 0) & (Q_BLOCK - 1)
    ci = lax.broadcasted_iota(jnp.int32, (2 * Q_BLOCK, 2 * Q_BLOCK), 1)
    band_two = jnp.where(ci < Q_BLOCK, ci - ri, ri + Q_BLOCK - ci) >= 0
    band_one = (lax.broadcasted_iota(jnp.int32, (2 * Q_BLOCK, Q_BLOCK), 1)
                <= lax.broadcasted_iota(jnp.int32, (2 * Q_BLOCK, Q_BLOCK), 0) & (Q_BLOCK - 1))

    def rows(start, dil):
        if dil == 1:
            return pl.ds(start if isinstance(start, int) else pl.multiple_of(start, Q_BLOCK), Q_BLOCK)
        return pl.ds(start, Q_BLOCK, stride=dil)

    def load(dil, start, with_prev):
        cur = rows(start, dil)
        q = q_ref[0, cur, :]
        q2 = jnp.concatenate([jnp.where(head0, q, 0.0), jnp.where(head0, 0.0, q)], axis=0).astype(BF16)
        k = k_ref[0, cur, :].astype(BF16)
        v = v_ref[0, cur, :].astype(BF16)
        if with_prev:
            prev = rows(start - dil * Q_BLOCK, dil)
            k = jnp.concatenate([k_ref[0, prev, :].astype(BF16), k], axis=0)
            v = jnp.concatenate([v_ref[0, prev, :].astype(BF16), v], axis=0)
        return q2, k, v

    def attend(q2, k, v, with_prev):
        n_keys = k.shape[0]
        s = lax.dot_general(q2, k, (((1,), (1,)), ((), ())), preferred_element_type=F32)
        s = jnp.where(band_two if with_prev else band_one, s, NEG)
        m = jnp.max(s, axis=-1, keepdims=True)
        p = jnp.exp2(s - m).astype(BF16)
        r = _dot(p, jnp.concatenate([v, jnp.ones((n_keys, LANES), BF16)], axis=1))
        den = r[:, LANES:]
        o = r[:, :LANES] * (1.0 / den)
        lse = m + jnp.log2(den)
        return jnp.where(head0, o[:Q_BLOCK], o[Q_BLOCK:]), jnp.where(head0, lse[:Q_BLOCK], lse[Q_BLOCK:])

    def blocks(branch, dil, starts, with_prev):
        loaded = [load(dil, st, with_prev) for st in starts]
        results = [attend(*ld, with_prev) for ld in loaded]
        for st, (o, lse) in zip(starts, results):
            ob_scr[branch, rows(st, dil), :] = o
            lse_scr[branch, rows(st, dil), :] = lse

    def grouped_loop(total, body):
        n_full = total // ATTN_GROUP

        def loop_body(it, carry):
            body([it * ATTN_GROUP + g for g in range(ATTN_GROUP)])
            return carry

        if n_full:
            lax.fori_loop(0, n_full, loop_body, 0)
        if total % ATTN_GROUP:
            body(list(range(n_full * ATTN_GROUP, total)))

    for branch, (window, dil) in enumerate(BRANCHES):
        assert window // dil == Q_BLOCK
        sub_len = seq // dil
        n_blocks = sub_len // Q_BLOCK

        def first_body(items, branch=branch, dil=dil):
            blocks(branch, dil, items, False)

        grouped_loop(dil, first_body)
        if n_blocks > 1:
            def rest_body(items, branch=branch, dil=dil, n_blocks=n_blocks):
                starts = [it // (n_blocks - 1) + dil * Q_BLOCK * (it % (n_blocks - 1) + 1) for it in items]
                blocks(branch, dil, starts, True)

            grouped_loop(dil * (n_blocks - 1), rest_body)

    rows_per_step = 256

    def merge_body(it, carry):
        sl = pl.ds(pl.multiple_of(it * rows_per_step, rows_per_step), rows_per_step)
        lses = [lse_scr[b, sl, :] for b in range(len(BRANCHES))]
        m = functools.reduce(jnp.maximum, lses)
        ws = [jnp.exp2(l - m) for l in lses]
        num = sum(w * ob_scr[b, sl, :] for b, w in enumerate(ws))
        o_ref[0, sl, :] = (num * (1.0 / sum(ws))).astype(o_ref.dtype)
        return carry

    lax.fori_loop(0, seq // rows_per_step, merge_body, 0)


def _attn_prompt(q, k, v):
    b, seq, _ = q.shape
    spec = pl.BlockSpec((1, seq, LANES), lambda i, j: (i, 0, j))
    return pl.pallas_call(
        functools.partial(_attn_prompt_kernel, seq=seq),
        grid=(b, A_WIDTH // LANES),
        in_specs=[spec, spec, spec],
        out_specs=spec,
        out_shape=jax.ShapeDtypeStruct((b, seq, A_WIDTH), BF16),
        scratch_shapes=[pltpu.VMEM((len(BRANCHES), seq, LANES), F32),
                        pltpu.VMEM((len(BRANCHES), seq, LANES), F32)],
        compiler_params=pltpu.CompilerParams(dimension_semantics=("arbitrary", "arbitrary"),
                                             vmem_limit_bytes=VMEM_LIMIT),
        name="attn_prompt",
    )(q, k, v)


def _attn_sample_kernel(q_ref, kn_ref, vn_ref, kt_ref, vt_ref, o_ref, *, win_buf):
    back = win_buf - lax.broadcasted_iota(jnp.int32, (1, win_buf), 1)
    masks = [jnp.where((back & (dil - 1)) == 0, back, window + 1) <= window for window, dil in BRANCHES]
    for h in range(N_HEADS):
        q = q_ref[h]
        k_new = kn_ref[h]
        v_new = vn_ref[h]
        s = jnp.sum(kt_ref[h] * q, axis=0, keepdims=True)
        s_new = jnp.sum(q * k_new, axis=0, keepdims=True)
        ps, pns, lses = [], [], []
        for mask in masks:
            sb = jnp.where(mask, s, NEG)
            m = jnp.maximum(jnp.max(sb, axis=1, keepdims=True), s_new)
            p = jnp.exp2(sb - m)
            pn = jnp.exp2(s_new - m)
            den = jnp.sum(p, axis=1, keepdims=True) + pn
            inv = 1.0 / den
            ps.append(p * inv)
            pns.append(pn * inv)
            lses.append(m + jnp.log2(den))
        mm = functools.reduce(jnp.maximum, lses)
        ws = [jnp.exp2(l - mm) for l in lses]
        inv = 1.0 / sum(ws)
        p_all = sum(w * p for w, p in zip(ws, ps)) * inv
        pn_all = sum(w * pn for w, pn in zip(ws, pns)) * inv
        o_ref[h] = jnp.sum(vt_ref[h] * p_all, axis=1, keepdims=True) + pn_all * v_new


def _attn_sample(q, k_new, v_new, cache_kt, cache_vt, layer):
    n, win_buf = cache_kt.shape[1], cache_kt.shape[-1]
    for window, dil in BRANCHES:
        assert window <= win_buf <= PAST_LEN and dil & (dil - 1) == 0
    col = pl.BlockSpec((None, N_HEADS, HEAD_DIM, 1), lambda i: (i, 0, 0, 0))
    cache = pl.BlockSpec((None, None, N_HEADS, HEAD_DIM, win_buf), lambda i: (layer, i, 0, 0, 0))
    as_cols = lambda a: a.reshape(n, N_HEADS, HEAD_DIM, 1)
    return pl.pallas_call(
        functools.partial(_attn_sample_kernel, win_buf=win_buf),
        grid=(n,),
        in_specs=[col, col, col, cache, cache],
        out_specs=col,
        out_shape=jax.ShapeDtypeStruct((n, N_HEADS, HEAD_DIM, 1), F32),
        compiler_params=pltpu.CompilerParams(dimension_semantics=("arbitrary",), vmem_limit_bytes=VMEM_LIMIT),
        name="attn_sample",
    )(as_cols(q), as_cols(k_new), as_cols(v_new), cache_kt, cache_vt).reshape(n, A_WIDTH)


def _mix_ffn_prompt_kernel(h_ref, att_ref, u_ref, vb_ref, wsp_ref, bsp_ref, wout_ref, gffn_ref, wup_ref, wdn_ref,
                           o_ref, gate_scr, *, tm):
    lane = lax.broadcasted_iota(jnp.int32, (1, LANES), 1)
    lo = lane < B_WIDTH // N_GROUPS
    ri = lax.broadcasted_iota(jnp.int32, (CHUNK, 2 * CHUNK), 0)
    ci = lax.broadcasted_iota(jnp.int32, (CHUNK, 2 * CHUNK), 1)
    causal = (ci & (CHUNK - 1)) <= ri
    n_pairs = B_WIDTH // LANES
    w_pairs = [jnp.where(causal, wsp_ref[p], 0.0).astype(BF16) for p in range(n_pairs)]
    for c in range(tm // CHUNK):
        rows = slice(c * CHUNK, (c + 1) * CHUNK)
        for p in range(n_pairs):
            sl = slice(p * LANES, (p + 1) * LANES)
            slab = vb_ref[rows, sl]
            rhs = jnp.concatenate([jnp.where(lo, slab, 0.0), jnp.where(lo, 0.0, slab)], axis=0).astype(BF16)
            mixed = _dot(w_pairs[p], rhs) + bsp_ref[:, sl]
            gate_scr[rows, sl] = (u_ref[rows, sl] * mixed).astype(BF16)
    mix = _dot(att_ref[...], wout_ref[0:A_WIDTH, :]) + _dot(gate_scr[...], wout_ref[A_WIDTH:, :])
    h1 = h_ref[...] + mix
    o_ref[...] = _ffn(h1, gffn_ref[...], wup_ref, wdn_ref)


def _mix_ffn_sample_kernel(h_ref, att_ref, u_ref, vb_ref, w0_ref, b0_ref, wout_ref, gffn_ref, wup_ref, wdn_ref,
                           o_ref):
    gate = (u_ref[...] * (vb_ref[...] * w0_ref[...] + b0_ref[...])).astype(BF16)
    mix = _dot(att_ref[...].astype(BF16), wout_ref[0:A_WIDTH, :]) + _dot(gate, wout_ref[A_WIDTH:, :])
    h1 = h_ref[...] + mix
    o_ref[...] = _ffn(h1, gffn_ref[...], wup_ref, wdn_ref)


def _mix_ffn(h, att, u, vb, gate_w, gate_b, wout_bf, j, gffn, wup_bf, wdn_bf, layer, tm, sample):
    t = h.shape[0]
    row = lambda i: (i, 0)
    if sample:
        kern = _mix_ffn_sample_kernel
        scratch = []
    else:
        kern = functools.partial(_mix_ffn_prompt_kernel, tm=tm)
        scratch = [pltpu.VMEM((tm, B_WIDTH), BF16)]
    return pl.pallas_call(
        kern,
        grid=(t // tm,),
        in_specs=[
            pl.BlockSpec((tm, D_MODEL), row), pl.BlockSpec((tm, A_WIDTH), row),
            pl.BlockSpec((tm, B_WIDTH), row), pl.BlockSpec((tm, B_WIDTH), row),
            _const_spec(gate_w.shape), _const_spec(gate_b.shape),
            _layer_spec((A_WIDTH + B_WIDTH, D_MODEL), j), _const_spec((1, D_MODEL)),
            _layer_spec((D_MODEL, D_FF), layer), _layer_spec((D_FF, D_MODEL), layer),
        ],
        out_specs=pl.BlockSpec((tm, D_MODEL), row),
        out_shape=jax.ShapeDtypeStruct((t, D_MODEL), F32),
        scratch_shapes=scratch,
        compiler_params=pltpu.CompilerParams(dimension_semantics=("arbitrary",), vmem_limit_bytes=VMEM_LIMIT),
        name="mix_ffn_sample" if sample else "mix_ffn",
    )(h, att, u, vb, gate_w, gate_b, wout_bf, gffn, wup_bf, wdn_bf)


def _glu_in(h, gmix, win_ref):
    n = _rmsnorm(h, gmix).astype(BF16)
    a = _dot(n, win_ref[:, 0:D_MODEL])
    gate = _dot(n, win_ref[:, D_MODEL:])
    return a * jax.nn.sigmoid(gate)


def _conv_tail(h, y, cg_ref, cb_ref, wout_ref, gffn_ref, wup_ref, wdn_ref):
    z = _layernorm(y, cg_ref[...], cb_ref[...])
    z = (z * jax.nn.sigmoid(z)).astype(BF16)
    h1 = h + _dot(z, wout_ref[...])
    return _ffn(h1, gffn_ref[...], wup_ref, wdn_ref)


def _conv_ffn_prompt_kernel(h_ref, gmix_ref, win_ref, wdw_ref, bdw_ref, cg_ref, cb_ref, wout_ref, gffn_ref,
                            wup_ref, wdn_ref, o_ref, cc_ref, xbuf, xs_scr, y_scr, h1_scr, *, tm, tiles_per_seq):
    g = pl.program_id(0)
    keep = CONV_WIDTH - 1
    n_slabs = D_MODEL // LANES
    n_chunks = D_FF // FF_CHUNK

    @pl.when(g == 0)
    def _():
        h1_scr[...] = jnp.zeros(h1_scr.shape, F32)
        xbuf[0:HALO, :] = jnp.zeros((HALO, D_MODEL), F32)

    s = jnp.minimum(g, pl.num_programs(0) - 2) % tiles_per_seq
    xbuf[0:HALO, :] = jnp.where(s == 0, 0.0, xbuf[0:HALO, :])
    h = h_ref[...]
    xbuf[HALO:HALO + tm, :] = _glu_in(h, gmix_ref[...], win_ref)

    shift_rows = tm + HALO - SUBLANES
    for b in range(1, SUBLANES):
        xs_scr[b - 1] = xbuf[b:b + shift_rows, :]

    def conv_slab(col, anchor):
        cs = slice(col * LANES, (col + 1) * LANES)
        taps = [wdw_ref[w:w + 1, cs] for w in range(CONV_WIDTH)]
        bias = jnp.broadcast_to(bdw_ref[:, cs], (CONV_ROWS, LANES))
        for r0 in range(0, tm, CONV_ROWS):
            acc = jnp.where(g >= 0, bias, anchor)
            for w in range(CONV_WIDTH):
                a, b = divmod(HALO - keep + w, SUBLANES)
                rows = slice(r0 + SUBLANES * a, r0 + SUBLANES * a + CONV_ROWS)
                x = xbuf[rows, cs] if b == 0 else xs_scr[b - 1, rows, cs]
                acc = acc + x * jnp.where(g >= 0, taps[w], anchor[0:1])
            y_scr[r0:r0 + CONV_ROWS, cs] = acc
            anchor = acc
        return anchor

    h1 = h1_scr[(g + 1) % 2]
    n = _rmsnorm(h1, gffn_ref[...]).astype(BF16)
    acc = h1
    slabs_per_chunk = n_slabs // n_chunks
    for c in range(n_chunks):
        a = jnp.maximum(_dot(n, wup_ref[:, c * FF_CHUNK:(c + 1) * FF_CHUNK]), 0.0)
        anchor = a[0:CONV_ROWS, 0:LANES]
        for i in range(slabs_per_chunk):
            anchor = conv_slab(c * slabs_per_chunk + i, anchor)
        acc = acc + _dot((a * a).astype(BF16), wdn_ref[c * FF_CHUNK:(c + 1) * FF_CHUNK, :])
    o_ref[...] = acc

    cc_ref[0] = xbuf[HALO + tm - keep:HALO + tm, :]
    xbuf[0:HALO, :] = xbuf[tm:tm + HALO, :]
    z = _layernorm(y_scr[...], cg_ref[...], cb_ref[...])
    z = (z * jax.nn.sigmoid(z)).astype(BF16)
    h1_scr[g % 2] = h + _dot(z, wout_ref[...])


def _conv_ffn_sample_kernel(h_ref, gmix_ref, win_ref, wdw_ref, bdw_ref, cg_ref, cb_ref, wout_ref, gffn_ref,
                            wup_ref, wdn_ref, st_ref, o_ref, cc_ref):
    keep = CONV_WIDTH - 1
    h = h_ref[...]
    x = _glu_in(h, gmix_ref[...], win_ref)
    y = x * wdw_ref[keep:keep + 1, :] + bdw_ref[...]
    for w in range(keep):
        y = y + st_ref[w] * wdw_ref[w:w + 1, :]
    for w in range(keep - 1):
        cc_ref[w] = st_ref[w + 1]
    cc_ref[keep - 1] = x
    o_ref[...] = _conv_tail(h, y, cg_ref, cb_ref, wout_ref, gffn_ref, wup_ref, wdn_ref)


def _conv_weight_specs(j, layer):
    return [
        _const_spec((1, D_MODEL)), _layer_spec((D_MODEL, 2 * D_MODEL), j), _const_spec((CONV_WIDTH, D_MODEL)),
        _const_spec((1, D_MODEL)), _const_spec((1, D_MODEL)), _const_spec((1, D_MODEL)),
        _layer_spec((D_MODEL, D_MODEL), j), _const_spec((1, D_MODEL)),
        _layer_spec((D_MODEL, D_FF), layer), _layer_spec((D_FF, D_MODEL), layer),
    ]


def _conv_ffn_prompt(h, weights, j, layer, batch, seq, tm):
    keep = CONV_WIDTH - 1
    n_s = seq // tm
    n_tiles = batch * n_s
    conv_tile = lambda g: (jnp.minimum(g, n_tiles - 1), 0)
    ffn_tile = lambda g: (jnp.maximum(g - 1, 0), 0)
    return pl.pallas_call(
        functools.partial(_conv_ffn_prompt_kernel, tm=tm, tiles_per_seq=n_s),
        grid=(n_tiles + 1,),
        in_specs=[pl.BlockSpec((tm, D_MODEL), conv_tile)] + _conv_weight_specs(j, layer),
        out_specs=[pl.BlockSpec((tm, D_MODEL), ffn_tile),
                   pl.BlockSpec((1, keep, D_MODEL), lambda g: (jnp.minimum(g, n_tiles - 1) // n_s, 0, 0))],
        out_shape=[jax.ShapeDtypeStruct((batch * seq, D_MODEL), F32),
                   jax.ShapeDtypeStruct((batch, keep, D_MODEL), F32)],
        scratch_shapes=[pltpu.VMEM((HALO + tm, D_MODEL), F32),
                        pltpu.VMEM((SUBLANES - 1, HALO + tm - SUBLANES, D_MODEL), F32),
                        pltpu.VMEM((tm, D_MODEL), F32),
                        pltpu.VMEM((2, tm, D_MODEL), F32)],
        compiler_params=pltpu.CompilerParams(dimension_semantics=("arbitrary",), vmem_limit_bytes=VMEM_LIMIT),
        name="conv_ffn",
    )(h, *weights)


def _conv_ffn_sample(h, weights, j, layer, state_t):
    n = h.shape[0]
    keep = CONV_WIDTH - 1
    return pl.pallas_call(
        _conv_ffn_sample_kernel,
        grid=(1,),
        in_specs=[_const_spec((n, D_MODEL))] + _conv_weight_specs(j, layer)
        + [pl.BlockSpec((None, keep, n, D_MODEL), lambda i: (j, 0, 0, 0))],
        out_specs=[_const_spec((n, D_MODEL)), _const_spec((keep, n, D_MODEL))],
        out_shape=[jax.ShapeDtypeStruct((n, D_MODEL), F32), jax.ShapeDtypeStruct((keep, n, D_MODEL), F32)],
        compiler_params=pltpu.CompilerParams(dimension_semantics=("arbitrary",), vmem_limit_bytes=VMEM_LIMIT),
        name="conv_ffn_sample",
    )(h, *weights, state_t)


def _rope_tables(pos):
    inv = ROPE_THETA ** (-jnp.arange(0, HEAD_DIM, 2, dtype=F32) / HEAD_DIM)
    ang = pos.astype(F32)[:, None] * inv[None, :]
    cos, sin = jnp.cos(ang), jnp.sin(ang)
    reps = LANES // HEAD_DIM
    return jnp.tile(jnp.concatenate([cos, cos], axis=-1), (1, reps)), jnp.tile(jnp.concatenate([-sin, sin], axis=-1), (1, reps))


def kernel(x_prompt, x_sample, cache_a_k, cache_a_v, state_c_conv, norm_mix_g, norm_ffn_g, w_ffn_up, w_ffn_down, w_in_ab, q_norm_g, k_norm_g, vb_norm_g, vb_norm_b, w_spatial, b_spatial, w_out_ab, w_c_in, w_c_dw, b_c_dw, c_norm_g, c_norm_b, w_c_out):
    batch, seq, _ = x_prompt.shape
    n_dec, dec_seq, _ = x_sample.shape
    depth = norm_mix_g.shape[0]
    n_ab, _, win_buf = cache_a_k.shape[:3]
    past_len = PAST_LEN
    assert dec_seq == 1 and seq % (BRANCHES[-1][1] * Q_BLOCK) == 0 and seq % CHUNK == 0
    tm = 256

    hp = x_prompt.reshape(batch * seq, D_MODEL)
    hs = x_sample.reshape(n_dec * dec_seq, D_MODEL)
    cos_p, sin_p = _rope_tables(jnp.arange(seq, dtype=jnp.int32))
    cos_s, sin_s = _rope_tables(jnp.full((n_dec,), past_len, dtype=jnp.int32))
    head_id = jnp.arange(LANES) // HEAD_DIM
    head_mean = jnp.where(head_id[:, None] == head_id[None, :], 1.0 / HEAD_DIM, 0.0).astype(BF16)
    cache_kt = cache_a_k.transpose(0, 1, 3, 4, 2)
    cache_vt = cache_a_v.transpose(0, 1, 3, 4, 2)
    state_t = state_c_conv.transpose(0, 2, 1, 3)
    group_dim = B_WIDTH // N_GROUPS
    row2 = lambda a: a.reshape(1, -1)

    wup, wdn = w_ffn_up.astype(BF16), w_ffn_down.astype(BF16)
    w_in, wout = w_in_ab.astype(BF16), w_out_ab.astype(BF16)
    wc_in, wc_out = w_c_in.astype(BF16), w_c_out.astype(BF16)

    ak_s, av_s, bv_p, bv_s, cc_p, cc_s = [], [], [], [], [], []
    kv_t = ()
    for layer in range(depth):
        j = layer // 2
        gffn = row2(norm_ffn_g[layer])
        gmix = row2(norm_mix_g[layer])
        if layer % 2 == 0:
            qg = row2(jnp.tile(q_norm_g[j], N_HEADS))
            kg = row2(jnp.tile(k_norm_g[j], N_HEADS))
            vbg, vbb = row2(vb_norm_g[j]), row2(vb_norm_b[j])
            qp, kp, vp, up, vbp, *kv_t = _ab_in(hp, gmix, w_in, j, cos_p, sin_p, qg, kg, head_mean, vbg, vbb, tm,
                                                seq=seq, prev_t=tuple(kv_t))
            qs, ks, vs, us, vbs = _ab_in(hs, gmix, w_in, j, cos_s, sin_s, qg, kg, head_mean, vbg, vbb, n_dec)
            shp = (batch, seq, A_WIDTH)
            att_p = _attn_prompt(qp.reshape(shp), kp.reshape(shp), vp.reshape(shp)).reshape(batch * seq, A_WIDTH)
            att_s = _attn_sample(qs, ks, vs, cache_kt, cache_vt, j)
            w_pairs = w_spatial[j].reshape(N_GROUPS // 2, 2, CHUNK, CHUNK).transpose(0, 2, 1, 3).reshape(N_GROUPS // 2, CHUNK, 2 * CHUNK)
            b_rows = jnp.repeat(b_spatial[j].T, group_dim, axis=1)
            w0 = row2(jnp.repeat(w_spatial[j][:, 0, 0], group_dim))
            b0 = row2(jnp.repeat(b_spatial[j][:, 0], group_dim))
            hp = _mix_ffn(hp, att_p, up, vbp, w_pairs, b_rows, wout, j, gffn, wup, wdn, layer, tm, False)
            hs = _mix_ffn(hs, att_s, us, vbs, w0, b0, wout, j, gffn, wup, wdn, layer, n_dec, True)
            ak_s.append(ks.reshape(n_dec, dec_seq, N_HEADS, HEAD_DIM))
            av_s.append(vs.reshape(n_dec, dec_seq, N_HEADS, HEAD_DIM))
            last_chunk_start = ((seq - 1) // CHUNK) * CHUNK
            bv_p.append(vbp.reshape(batch, seq, B_WIDTH)[:, last_chunk_start:])
            bv_s.append(vbs.reshape(n_dec, dec_seq, B_WIDTH))
        else:
            weights = (gmix, wc_in, w_c_dw[j], row2(b_c_dw[j]), row2(c_norm_g[j]),
                       row2(c_norm_b[j]), wc_out, gffn, wup, wdn)
            hp, new_cp = _conv_ffn_prompt(hp, weights, j, layer, batch, seq, tm)
            hs, new_cs = _conv_ffn_sample(hs, weights, j, layer, state_t)
            cc_p.append(new_cp)
            cc_s.append(new_cs)

    prompt_buf = min(BRANCHES[-1][0], seq)
    ak_p, av_p = [a.reshape(n_ab, batch, N_HEADS, HEAD_DIM, seq).transpose(0, 1, 4, 2, 3)[:, :, seq - prompt_buf:]
                  for a in kv_t]
    return (hp.reshape(batch, seq, D_MODEL), hs.reshape(n_dec, dec_seq, D_MODEL),
            ak_p, av_p, jnp.stack(ak_s), jnp.stack(av_s),
            jnp.stack(bv_p), jnp.stack(bv_s), jnp.stack(cc_p), jnp.stack(cc_s).transpose(0, 2, 1, 3))
```

```python
import functools

import jax
import jax.numpy as jnp
from jax import lax
from jax.experimental import pallas as pl
from jax.experimental.pallas import tpu as pltpu

F32 = jnp.float32
BF16 = jnp.bfloat16

D_MODEL = 1024
N_HEADS = 8
HEAD_DIM = 64
A_WIDTH = N_HEADS * HEAD_DIM
B_WIDTH = 512
N_GROUPS = 8
CHUNK = 128
Q_BLOCK = 128
BRANCHES = ((128, 1), (512, 4), (2048, 16))
ROPE_THETA = 10000.0
CONV_WIDTH = 31
PAST_LEN = 8192
D_FF = 4 * D_MODEL
EPS = 1e-6
IN_AB = 3 * A_WIDTH + 2 * B_WIDTH

LANES = 128
SUBLANES = 8
HALO = 32
CONV_ROWS = 64
FF_CHUNK = 1024
NEG = -1e30
LOG2_E = 1.4426950408889634
QK_SCALE = HEAD_DIM ** -0.5 * LOG2_E
ATTN_GROUP = 16
VMEM_LIMIT = 56 * 1024 * 1024


def _rmsnorm(x, g):
    ms = jnp.mean(x * x, axis=-1, keepdims=True)
    return x * lax.rsqrt(ms + EPS) * g


def _layernorm(x, g, b):
    mu = jnp.mean(x, axis=-1, keepdims=True)
    xc = x - mu
    var = jnp.mean(xc * xc, axis=-1, keepdims=True)
    return xc * lax.rsqrt(var + EPS) * g + b


def _dot(a, b):
    return jnp.dot(a, b, preferred_element_type=F32)


def _ffn(h1, g, wup_ref, wdn_ref):
    n = _rmsnorm(h1, g).astype(BF16)
    acc = h1
    for c in range(D_FF // FF_CHUNK):
        a = _dot(n, wup_ref[:, c * FF_CHUNK:(c + 1) * FF_CHUNK])
        a = jnp.maximum(a, 0.0)
        a = (a * a).astype(BF16)
        acc = acc + _dot(a, wdn_ref[c * FF_CHUNK:(c + 1) * FF_CHUNK, :])
    return acc


def _ab_in_kernel(h_ref, g_ref, w_ref, cos_ref, sin_ref, qg_ref, kg_ref, hm_ref, vbg_ref, vbb_ref, *refs,
                  n_prev, transposed):
    prev_refs, refs = refs[:2 if n_prev else 0], refs[2 if n_prev else 0:]
    q_ref, k_ref, v_ref, u_ref, vb_ref = refs[:5]
    kt_ref, vt_ref = refs[5:] if transposed else (None, None)
    if n_prev:
        kt_ref[0:n_prev] = prev_refs[0][...]
        vt_ref[0:n_prev] = prev_refs[1][...]
    n = _rmsnorm(h_ref[...], g_ref[...]).astype(BF16)
    cos = cos_ref[...]
    sin = sin_ref[...]
    hm = hm_ref[...]
    lane = lax.broadcasted_iota(jnp.int32, (1, LANES), 1)
    first_half = (lane & (HEAD_DIM // 2)) == 0

    def head_norm_rope(z, gain_ref, out_ref, scale, t_ref):
        for c in range(A_WIDTH // LANES):
            sl = slice(c * LANES, (c + 1) * LANES)
            zc = z[:, sl]
            sq = zc * zc
            hi = sq.astype(BF16)
            lo = (sq - hi.astype(F32)).astype(BF16)
            ms = _dot(hi, hm) + _dot(lo, hm)
            y = zc * lax.rsqrt(ms + EPS) * gain_ref[:, sl]
            partner = jnp.where(first_half, pltpu.roll(y, LANES - HEAD_DIM // 2, 1),
                                pltpu.roll(y, HEAD_DIM // 2, 1))
            out = y * cos + partner * sin
            if scale != 1.0:
                out = out * scale
            out_ref[:, sl] = out
            if t_ref is not None:
                t_ref[n_prev, sl, :] = out.T

    head_norm_rope(_dot(n, w_ref[:, 0:A_WIDTH]), qg_ref, q_ref, QK_SCALE, None)
    head_norm_rope(_dot(n, w_ref[:, A_WIDTH:2 * A_WIDTH]), kg_ref, k_ref, 1.0, kt_ref)
    v = _dot(n, w_ref[:, 2 * A_WIDTH:3 * A_WIDTH])
    v_ref[...] = v
    if transposed:
        for c in range(A_WIDTH // LANES):
            sl = slice(c * LANES, (c + 1) * LANES)
            vt_ref[n_prev, sl, :] = v[:, sl].T
    u_ref[...] = jax.nn.gelu(_dot(n, w_ref[:, 3 * A_WIDTH:3 * A_WIDTH + B_WIDTH]))
    vb = jax.nn.gelu(_dot(n, w_ref[:, 3 * A_WIDTH + B_WIDTH:IN_AB]))
    vb_ref[...] = _layernorm(vb, vbg_ref[...], vbb_ref[...])


def _const_spec(shape):
    nd = len(shape)
    return pl.BlockSpec(shape, lambda *_: (0,) * nd, pipeline_mode=pl.Buffered(1))


def _layer_spec(shape, layer):
    nd = len(shape)
    return pl.BlockSpec((None, *shape), lambda *_: (layer,) + (0,) * nd, pipeline_mode=pl.Buffered(1))


def _ab_in(h, g, w_bf, j, cos_t, sin_t, qg, kg, hm, vbg, vbb, tm, seq=None, prev_t=()):
    t = h.shape[0]
    n_pos_blocks = cos_t.shape[0] // tm
    row = lambda i: (i, 0)
    tab = lambda i: (i % n_pos_blocks, 0)
    out = jax.ShapeDtypeStruct((t, A_WIDTH), F32)
    in_specs = [
        pl.BlockSpec((tm, D_MODEL), row), _const_spec((1, D_MODEL)), _layer_spec((D_MODEL, IN_AB), j),
        pl.BlockSpec((tm, LANES), tab), pl.BlockSpec((tm, LANES), tab),
        _const_spec((1, A_WIDTH)), _const_spec((1, A_WIDTH)), _const_spec((LANES, LANES)),
        _const_spec((1, B_WIDTH)), _const_spec((1, B_WIDTH)),
    ]
    out_specs = [pl.BlockSpec((tm, A_WIDTH), row)] * 5
    out_shape = [out] * 5
    n_prev = prev_t[0].shape[0] if prev_t else 0
    if seq is not None:
        tiles_per_seq = seq // tm
        t_map = lambda i: (0, i // tiles_per_seq, 0, i % tiles_per_seq)
        if prev_t:
            in_specs += [pl.BlockSpec((n_prev, None, A_WIDTH, tm), t_map)] * 2
        out_specs += [pl.BlockSpec((n_prev + 1, None, A_WIDTH, tm), t_map)] * 2
        out_shape += [jax.ShapeDtypeStruct((n_prev + 1, t // seq, A_WIDTH, seq), F32)] * 2
    return pl.pallas_call(
        functools.partial(_ab_in_kernel, n_prev=n_prev, transposed=seq is not None),
        grid=(t // tm,),
        in_specs=in_specs,
        out_specs=out_specs,
        out_shape=out_shape,
        compiler_params=pltpu.CompilerParams(dimension_semantics=("arbitrary",), vmem_limit_bytes=VMEM_LIMIT),
        name="ab_in",
    )(h, g, w_bf, cos_t, sin_t, qg, kg, hm, vbg, vbb, *prev_t)


def _attn_prompt_kernel(q_ref, k_ref, v_ref, o_ref, ob_scr, lse_scr, *, seq):
    lane = lax.broadcasted_iota(jnp.int32, (1, LANES), 1)
    head0 = lane < HEAD_DIM
    ri = lax.broadcasted_iota(jnp.int32, (2 * Q_BLOCK, 2 * Q_BLOCK), 0) & (Q_BLOCK - 1)
    ci = lax.broadcasted_iota(jnp.int32, (2 * Q_BLOCK, 2 * Q_BLOCK), 1)
    band_two = jnp.where(ci < Q_BLOCK, ci - ri, ri + Q_BLOCK - ci) >= 0
    band_one = (lax.broadcasted_iota(jnp.int32, (2 * Q_BLOCK, Q_BLOCK), 1)
                <= lax.broadcasted_iota(jnp.int32, (2 * Q_BLOCK, Q_BLOCK), 0) & (Q_BLOCK - 1))

    def rows(start, dil):
        if dil == 1:
            return pl.ds(start if isinstance(start, int) else pl.multiple_of(start, Q_BLOCK), Q_BLOCK)
        return pl.ds(start, Q_BLOCK, stride=dil)

    def load(dil, start, with_prev):
        cur = rows(start, dil)
        q = q_ref[0, cur, :]
        q2 = jnp.concatenate([jnp.where(head0, q, 0.0), jnp.where(head0, 0.0, q)], axis=0).astype(BF16)
        k = k_ref[0, cur, :].astype(BF16)
        v = v_ref[0, cur, :].astype(BF16)
        if with_prev:
            prev = rows(start - dil * Q_BLOCK, dil)
            k = jnp.concatenate([k_ref[0, prev, :].astype(BF16), k], axis=0)
            v = jnp.concatenate([v_ref[0, prev, :].astype(BF16), v], axis=0)
        return q2, k, v

    def attend(q2, k, v, with_prev):
        n_keys = k.shape[0]
        s = lax.dot_general(q2, k, (((1,), (1,)), ((), ())), preferred_element_type=F32)
        s = jnp.where(band_two if with_prev else band_one, s, NEG)
        m = jnp.max(s, axis=-1, keepdims=True)
        p = jnp.exp2(s - m).astype(BF16)
        r = _dot(p, jnp.concatenate([v, jnp.ones((n_keys, LANES), BF16)], axis=1))
        den = r[:, LANES:]
        o = r[:, :LANES] * (1.0 / den)
        lse = m + jnp.log2(den)
        return jnp.where(head0, o[:Q_BLOCK], o[Q_BLOCK:]), jnp.where(head0, lse[:Q_BLOCK], lse[Q_BLOCK:])

    def blocks(branch, dil, starts, with_prev):
        loaded = [load(dil, st, with_prev) for st in starts]
        results = [attend(*ld, with_prev) for ld in loaded]
        for st, (o, lse) in zip(starts, results):
            ob_scr[branch, rows(st, dil), :] = o
            lse_scr[branch, rows(st, dil), :] = lse

    def grouped_loop(total, body):
        n_full = total // ATTN_GROUP

        def loop_body(it, carry):
            body([it * ATTN_GROUP + g for g in range(ATTN_GROUP)])
            return carry

        if n_full:
            lax.fori_loop(0, n_full, loop_body, 0)
        if total % ATTN_GROUP:
            body(list(range(n_full * ATTN_GROUP, total)))

    for branch, (window, dil) in enumerate(BRANCHES):
        assert window // dil == Q_BLOCK
        sub_len = seq // dil
        n_blocks = sub_len // Q_BLOCK

        def first_body(items, branch=branch, dil=dil):
            blocks(branch, dil, items, False)

        grouped_loop(dil, first_body)
        if n_blocks > 1:
            def rest_body(items, branch=branch, dil=dil, n_blocks=n_blocks):
                starts = [it // (n_blocks - 1) + dil * Q_BLOCK * (it % (n_blocks - 1) + 1) for it in items]
                blocks(branch, dil, starts, True)

            grouped_loop(dil * (n_blocks - 1), rest_body)

    rows_per_step = 256

    def merge_body(it, carry):
        sl = pl.ds(pl.multiple_of(it * rows_per_step, rows_per_step), rows_per_step)
        lses = [lse_scr[b, sl, :] for b in range(len(BRANCHES))]
        m = functools.reduce(jnp.maximum, lses)
        ws = [jnp.exp2(l - m) for l in lses]
        num = sum(w * ob_scr[b, sl, :] for b, w in enumerate(ws))
        o_ref[0, sl, :] = (num * (1.0 / sum(ws))).astype(o_ref.dtype)
        return carry

    lax.fori_loop(0, seq // rows_per_step, merge_body, 0)


def _attn_prompt(q, k, v):
    b, seq, _ = q.shape
    spec = pl.BlockSpec((1, seq, LANES), lambda i, j: (i, 0, j))
    return pl.pallas_call(
        functools.partial(_attn_prompt_kernel, seq=seq),
        grid=(b, A_WIDTH // LANES),
        in_specs=[spec, spec, spec],
        out_specs=spec,
        out_shape=jax.ShapeDtypeStruct((b, seq, A_WIDTH), BF16),
        scratch_shapes=[pltpu.VMEM((len(BRANCHES), seq, LANES), F32),
                        pltpu.VMEM((len(BRANCHES), seq, LANES), F32)],
        compiler_params=pltpu.CompilerParams(dimension_semantics=("arbitrary", "arbitrary"),
                                             vmem_limit_bytes=VMEM_LIMIT),
        name="attn_prompt",
    )(q, k, v)


def _attn_sample_kernel(q_ref, kn_ref, vn_ref, kt_ref, vt_ref, o_ref, *, win_buf):
    back = win_buf - lax.broadcasted_iota(jnp.int32, (1, win_buf), 1)
    masks = [jnp.where((back & (dil - 1)) == 0, back, window + 1) <= window for window, dil in BRANCHES]
    for h in range(N_HEADS):
        q = q_ref[h]
        k_new = kn_ref[h]
        v_new = vn_ref[h]
        s = jnp.sum(kt_ref[h] * q, axis=0, keepdims=True)
        s_new = jnp.sum(q * k_new, axis=0, keepdims=True)
        ps, pns, lses = [], [], []
        for mask in masks:
            sb = jnp.where(mask, s, NEG)
            m = jnp.maximum(jnp.max(sb, axis=1, keepdims=True), s_new)
            p = jnp.exp2(sb - m)
            pn = jnp.exp2(s_new - m)
            den = jnp.sum(p, axis=1, keepdims=True) + pn
            inv = 1.0 / den
            ps.append(p * inv)
            pns.append(pn * inv)
            lses.append(m + jnp.log2(den))
        mm = functools.reduce(jnp.maximum, lses)
        ws = [jnp.exp2(l - mm) for l in lses]
        inv = 1.0 / sum(ws)
        p_all = sum(w * p for w, p in zip(ws, ps)) * inv
        pn_all = sum(w * pn for w, pn in zip(ws, pns)) * inv
        o_ref[h] = jnp.sum(vt_ref[h] * p_all, axis=1, keepdims=True) + pn_all * v_new


def _attn_sample(q, k_new, v_new, cache_kt, cache_vt, layer):
    n, win_buf = cache_kt.shape[1], cache_kt.shape[-1]
    for window, dil in BRANCHES:
        assert window <= win_buf <= PAST_LEN and dil & (dil - 1) == 0
    col = pl.BlockSpec((None, N_HEADS, HEAD_DIM, 1), lambda i: (i, 0, 0, 0))
    cache = pl.BlockSpec((None, None, N_HEADS, HEAD_DIM, win_buf), lambda i: (layer, i, 0, 0, 0))
    as_cols = lambda a: a.reshape(n, N_HEADS, HEAD_DIM, 1)
    return pl.pallas_call(
        functools.partial(_attn_sample_kernel, win_buf=win_buf),
        grid=(n,),
        in_specs=[col, col, col, cache, cache],
        out_specs=col,
        out_shape=jax.ShapeDtypeStruct((n, N_HEADS, HEAD_DIM, 1), F32),
        compiler_params=pltpu.CompilerParams(dimension_semantics=("arbitrary",), vmem_limit_bytes=VMEM_LIMIT),
        name="attn_sample",
    )(as_cols(q), as_cols(k_new), as_cols(v_new), cache_kt, cache_vt).reshape(n, A_WIDTH)


def _mix_ffn_prompt_kernel(h_ref, att_ref, u_ref, vb_ref, wsp_ref, bsp_ref, wout_ref, gffn_ref, wup_ref, wdn_ref,
                           o_ref, gate_scr, *, tm):
    lane = lax.broadcasted_iota(jnp.int32, (1, LANES), 1)
    lo = lane < B_WIDTH // N_GROUPS
    ri = lax.broadcasted_iota(jnp.int32, (CHUNK, 2 * CHUNK), 0)
    ci = lax.broadcasted_iota(jnp.int32, (CHUNK, 2 * CHUNK), 1)
    causal = (ci & (CHUNK - 1)) <= ri
    n_pairs = B_WIDTH // LANES
    w_pairs = [jnp.where(causal, wsp_ref[p], 0.0).astype(BF16) for p in range(n_pairs)]
    for c in range(tm // CHUNK):
        rows = slice(c * CHUNK, (c + 1) * CHUNK)
        for p in range(n_pairs):
            sl = slice(p * LANES, (p + 1) * LANES)
            slab = vb_ref[rows, sl]
            rhs = jnp.concatenate([jnp.where(lo, slab, 0.0), jnp.where(lo, 0.0, slab)], axis=0).astype(BF16)
            mixed = _dot(w_pairs[p], rhs) + bsp_ref[:, sl]
            gate_scr[rows, sl] = (u_ref[rows, sl] * mixed).astype(BF16)
    mix = _dot(att_ref[...], wout_ref[0:A_WIDTH, :]) + _dot(gate_scr[...], wout_ref[A_WIDTH:, :])
    h1 = h_ref[...] + mix
    o_ref[...] = _ffn(h1, gffn_ref[...], wup_ref, wdn_ref)


def _mix_ffn_sample_kernel(h_ref, att_ref, u_ref, vb_ref, w0_ref, b0_ref, wout_ref, gffn_ref, wup_ref, wdn_ref,
                           o_ref):
    gate = (u_ref[...] * (vb_ref[...] * w0_ref[...] + b0_ref[...])).astype(BF16)
    mix = _dot(att_ref[...].astype(BF16), wout_ref[0:A_WIDTH, :]) + _dot(gate, wout_ref[A_WIDTH:, :])
    h1 = h_ref[...] + mix
    o_ref[...] = _ffn(h1, gffn_ref[...], wup_ref, wdn_ref)


def _mix_ffn(h, att, u, vb, gate_w, gate_b, wout_bf, j, gffn, wup_bf, wdn_bf, layer, tm, sample):
    t = h.shape[0]
    row = lambda i: (i, 0)
    if sample:
        kern = _mix_ffn_sample_kernel
        scratch = []
    else:
        kern = functools.partial(_mix_ffn_prompt_kernel, tm=tm)
        scratch = [pltpu.VMEM((tm, B_WIDTH), BF16)]
    return pl.pallas_call(
        kern,
        grid=(t // tm,),
        in_specs=[
            pl.BlockSpec((tm, D_MODEL), row), pl.BlockSpec((tm, A_WIDTH), row),
            pl.BlockSpec((tm, B_WIDTH), row), pl.BlockSpec((tm, B_WIDTH), row),
            _const_spec(gate_w.shape), _const_spec(gate_b.shape),
            _layer_spec((A_WIDTH + B_WIDTH, D_MODEL), j), _const_spec((1, D_MODEL)),
            _layer_spec((D_MODEL, D_FF), layer), _layer_spec((D_FF, D_MODEL), layer),
        ],
        out_specs=pl.BlockSpec((tm, D_MODEL), row),
        out_shape=jax.ShapeDtypeStruct((t, D_MODEL), F32),
        scratch_shapes=scratch,
        compiler_params=pltpu.CompilerParams(dimension_semantics=("arbitrary",), vmem_limit_bytes=VMEM_LIMIT),
        name="mix_ffn_sample" if sample else "mix_ffn",
    )(h, att, u, vb, gate_w, gate_b, wout_bf, gffn, wup_bf, wdn_bf)


def _glu_in(h, gmix, win_ref):
    n = _rmsnorm(h, gmix).astype(BF16)
    a = _dot(n, win_ref[:, 0:D_MODEL])
    gate = _dot(n, win_ref[:, D_MODEL:])
    return a * jax.nn.sigmoid(gate)


def _conv_tail(h, y, cg_ref, cb_ref, wout_ref, gffn_ref, wup_ref, wdn_ref):
    z = _layernorm(y, cg_ref[...], cb_ref[...])
    z = (z * jax.nn.sigmoid(z)).astype(BF16)
    h1 = h + _dot(z, wout_ref[...])
    return _ffn(h1, gffn_ref[...], wup_ref, wdn_ref)


def _conv_ffn_prompt_kernel(h_ref, gmix_ref, win_ref, wdw_ref, bdw_ref, cg_ref, cb_ref, wout_ref, gffn_ref,
                            wup_ref, wdn_ref, o_ref, cc_ref, xbuf, xs_scr, y_scr, h1_scr, *, tm, tiles_per_seq):
    g = pl.program_id(0)
    keep = CONV_WIDTH - 1
    n_slabs = D_MODEL // LANES
    n_chunks = D_FF // FF_CHUNK

    @pl.when(g == 0)
    def _():
        h1_scr[...] = jnp.zeros(h1_scr.shape, F32)
        xbuf[0:HALO, :] = jnp.zeros((HALO, D_MODEL), F32)

    s = jnp.minimum(g, pl.num_programs(0) - 2) % tiles_per_seq
    xbuf[0:HALO, :] = jnp.where(s == 0, 0.0, xbuf[0:HALO, :])
    h = h_ref[...]
    xbuf[HALO:HALO + tm, :] = _glu_in(h, gmix_ref[...], win_ref)

    shift_rows = tm + HALO - SUBLANES
    for b in range(1, SUBLANES):
        xs_scr[b - 1] = xbuf[b:b + shift_rows, :]

    def conv_slab(col, anchor):
        cs = slice(col * LANES, (col + 1) * LANES)
        taps = [wdw_ref[w:w + 1, cs] for w in range(CONV_WIDTH)]
        bias = jnp.broadcast_to(bdw_ref[:, cs], (CONV_ROWS, LANES))
        for r0 in range(0, tm, CONV_ROWS):
            acc = jnp.where(g >= 0, bias, anchor)
            for w in range(CONV_WIDTH):
                a, b = divmod(HALO - keep + w, SUBLANES)
                rows = slice(r0 + SUBLANES * a, r0 + SUBLANES * a + CONV_ROWS)
                x = xbuf[rows, cs] if b == 0 else xs_scr[b - 1, rows, cs]
                acc = acc + x * jnp.where(g >= 0, taps[w], anchor[0:1])
            y_scr[r0:r0 + CONV_ROWS, cs] = acc
            anchor = acc
        return anchor

    h1 = h1_scr[(g + 1) % 2]
    n = _rmsnorm(h1, gffn_ref[...]).astype(BF16)
    acc = h1
    slabs_per_chunk = n_slabs // n_chunks
    for c in range(n_chunks):
        a = jnp.maximum(_dot(n, wup_ref[:, c * FF_CHUNK:(c + 1) * FF_CHUNK]), 0.0)
        anchor = a[0:CONV_ROWS, 0:LANES]
        for i in range(slabs_per_chunk):
            anchor = conv_slab(c * slabs_per_chunk + i, anchor)
        acc = acc + _dot((a * a).astype(BF16), wdn_ref[c * FF_CHUNK:(c + 1) * FF_CHUNK, :])
    o_ref[...] = acc

    cc_ref[0] = xbuf[HALO + tm - keep:HALO + tm, :]
    xbuf[0:HALO, :] = xbuf[tm:tm + HALO, :]
    z = _layernorm(y_scr[...], cg_ref[...], cb_ref[...])
    z = (z * jax.nn.sigmoid(z)).astype(BF16)
    h1_scr[g % 2] = h + _dot(z, wout_ref[...])


def _conv_ffn_sample_kernel(h_ref, gmix_ref, win_ref, wdw_ref, bdw_ref, cg_ref, cb_ref, wout_ref, gffn_ref,
                            wup_ref, wdn_ref, st_ref, o_ref, cc_ref):
    keep = CONV_WIDTH - 1
    h = h_ref[...]
    x = _glu_in(h, gmix_ref[...], win_ref)
    y = x * wdw_ref[keep:keep + 1, :] + bdw_ref[...]
    for w in range(keep):
        y = y + st_ref[w] * wdw_ref[w:w + 1, :]
    for w in range(keep - 1):
        cc_ref[w] = st_ref[w + 1]
    cc_ref[keep - 1] = x
    o_ref[...] = _conv_tail(h, y, cg_ref, cb_ref, wout_ref, gffn_ref, wup_ref, wdn_ref)


def _conv_weight_specs(j, layer):
    return [
        _const_spec((1, D_MODEL)), _layer_spec((D_MODEL, 2 * D_MODEL), j), _const_spec((CONV_WIDTH, D_MODEL)),
        _const_spec((1, D_MODEL)), _const_spec((1, D_MODEL)), _const_spec((1, D_MODEL)),
        _layer_spec((D_MODEL, D_MODEL), j), _const_spec((1, D_MODEL)),
        _layer_spec((D_MODEL, D_FF), layer), _layer_spec((D_FF, D_MODEL), layer),
    ]


def _conv_ffn_prompt(h, weights, j, layer, batch, seq, tm):
    keep = CONV_WIDTH - 1
    n_s = seq // tm
    n_tiles = batch * n_s
    conv_tile = lambda g: (jnp.minimum(g, n_tiles - 1), 0)
    ffn_tile = lambda g: (jnp.maximum(g - 1, 0), 0)
    return pl.pallas_call(
        functools.partial(_conv_ffn_prompt_kernel, tm=tm, tiles_per_seq=n_s),
        grid=(n_tiles + 1,),
        in_specs=[pl.BlockSpec((tm, D_MODEL), conv_tile)] + _conv_weight_specs(j, layer),
        out_specs=[pl.BlockSpec((tm, D_MODEL), ffn_tile),
                   pl.BlockSpec((1, keep, D_MODEL), lambda g: (jnp.minimum(g, n_tiles - 1) // n_s, 0, 0))],
        out_shape=[jax.ShapeDtypeStruct((batch * seq, D_MODEL), F32),
                   jax.ShapeDtypeStruct((batch, keep, D_MODEL), F32)],
        scratch_shapes=[pltpu.VMEM((HALO + tm, D_MODEL), F32),
                        pltpu.VMEM((SUBLANES - 1, HALO + tm - SUBLANES, D_MODEL), F32),
                        pltpu.VMEM((tm, D_MODEL), F32),
                        pltpu.VMEM((2, tm, D_MODEL), F32)],
        compiler_params=pltpu.CompilerParams(dimension_semantics=("arbitrary",), vmem_limit_bytes=VMEM_LIMIT),
        name="conv_ffn",
    )(h, *weights)


def _conv_ffn_sample(h, weights, j, layer, state_t):
    n = h.shape[0]
    keep = CONV_WIDTH - 1
    return pl.pallas_call(
        _conv_ffn_sample_kernel,
        grid=(1,),
        in_specs=[_const_spec((n, D_MODEL))] + _conv_weight_specs(j, layer)
        + [pl.BlockSpec((None, keep, n, D_MODEL), lambda i: (j, 0, 0, 0))],
        out_specs=[_const_spec((n, D_MODEL)), _const_spec((keep, n, D_MODEL))],
        out_shape=[jax.ShapeDtypeStruct((n, D_MODEL), F32), jax.ShapeDtypeStruct((keep, n, D_MODEL), F32)],
        compiler_params=pltpu.CompilerParams(dimension_semantics=("arbitrary",), vmem_limit_bytes=VMEM_LIMIT),
        name="conv_ffn_sample",
    )(h, *weights, state_t)


def _rope_tables(pos):
    inv = ROPE_THETA ** (-jnp.arange(0, HEAD_DIM, 2, dtype=F32) / HEAD_DIM)
    ang = pos.astype(F32)[:, None] * inv[None, :]
    cos, sin = jnp.cos(ang), jnp.sin(ang)
    reps = LANES // HEAD_DIM
    return jnp.tile(jnp.concatenate([cos, cos], axis=-1), (1, reps)), jnp.tile(jnp.concatenate([-sin, sin], axis=-1), (1, reps))


def kernel(x_prompt, x_sample, cache_a_k, cache_a_v, state_c_conv, norm_mix_g, norm_ffn_g, w_ffn_up, w_ffn_down, w_in_ab, q_norm_g, k_norm_g, vb_norm_g, vb_norm_b, w_spatial, b_spatial, w_out_ab, w_c_in, w_c_dw, b_c_dw, c_norm_g, c_norm_b, w_c_out):
    batch, seq, _ = x_prompt.shape
    n_dec, dec_seq, _ = x_sample.shape
    depth = norm_mix_g.shape[0]
    n_ab, _, win_buf = cache_a_k.shape[:3]
    past_len = PAST_LEN
    assert dec_seq == 1 and seq % (BRANCHES[-1][1] * Q_BLOCK) == 0 and seq % CHUNK == 0
    tm = 256

    hp = x_prompt.reshape(batch * seq, D_MODEL)
    hs = x_sample.reshape(n_dec * dec_seq, D_MODEL)
    cos_p, sin_p = _rope_tables(jnp.arange(seq, dtype=jnp.int32))
    cos_s, sin_s = _rope_tables(jnp.full((n_dec,), past_len, dtype=jnp.int32))
    head_id = jnp.arange(LANES) // HEAD_DIM
    head_mean = jnp.where(head_id[:, None] == head_id[None, :], 1.0 / HEAD_DIM, 0.0).astype(BF16)
    cache_kt = cache_a_k.transpose(0, 1, 3, 4, 2)
    cache_vt = cache_a_v.transpose(0, 1, 3, 4, 2)
    state_t = state_c_conv.transpose(0, 2, 1, 3)
    group_dim = B_WIDTH // N_GROUPS
    row2 = lambda a: a.reshape(1, -1)

    wup, wdn = w_ffn_up.astype(BF16), w_ffn_down.astype(BF16)
    w_in, wout = w_in_ab.astype(BF16), w_out_ab.astype(BF16)
    wc_in, wc_out = w_c_in.astype(BF16), w_c_out.astype(BF16)

    ak_s, av_s, bv_p, bv_s, cc_p, cc_s = [], [], [], [], [], []
    kv_t = ()
    for layer in range(depth):
        j = layer // 2
        gffn = row2(norm_ffn_g[layer])
        gmix = row2(norm_mix_g[layer])
        if layer % 2 == 0:
            qg = row2(jnp.tile(q_norm_g[j], N_HEADS))
            kg = row2(jnp.tile(k_norm_g[j], N_HEADS))
            vbg, vbb = row2(vb_norm_g[j]), row2(vb_norm_b[j])
            qp, kp, vp, up, vbp, *kv_t = _ab_in(hp, gmix, w_in, j, cos_p, sin_p, qg, kg, head_mean, vbg, vbb, tm,
                                                seq=seq, prev_t=tuple(kv_t))
            qs, ks, vs, us, vbs = _ab_in(hs, gmix, w_in, j, cos_s, sin_s, qg, kg, head_mean, vbg, vbb, n_dec)
            shp = (batch, seq, A_WIDTH)
            att_p = _attn_prompt(qp.reshape(shp), kp.reshape(shp), vp.reshape(shp)).reshape(batch * seq, A_WIDTH)
            att_s = _attn_sample(qs, ks, vs, cache_kt, cache_vt, j)
            w_pairs = w_spatial[j].reshape(N_GROUPS // 2, 2, CHUNK, CHUNK).transpose(0, 2, 1, 3).reshape(N_GROUPS // 2, CHUNK, 2 * CHUNK)
            b_rows = jnp.repeat(b_spatial[j].T, group_dim, axis=1)
            w0 = row2(jnp.repeat(w_spatial[j][:, 0, 0], group_dim))
            b0 = row2(jnp.repeat(b_spatial[j][:, 0], group_dim))
            hp = _mix_ffn(hp, att_p, up, vbp, w_pairs, b_rows, wout, j, gffn, wup, wdn, layer, tm, False)
            hs = _mix_ffn(hs, att_s, us, vbs, w0, b0, wout, j, gffn, wup, wdn, layer, n_dec, True)
            ak_s.append(ks.reshape(n_dec, dec_seq, N_HEADS, HEAD_DIM))
            av_s.append(vs.reshape(n_dec, dec_seq, N_HEADS, HEAD_DIM))
            last_chunk_start = ((seq - 1) // CHUNK) * CHUNK
            bv_p.append(vbp.reshape(batch, seq, B_WIDTH)[:, last_chunk_start:])
            bv_s.append(vbs.reshape(n_dec, dec_seq, B_WIDTH))
        else:
            weights = (gmix, wc_in, w_c_dw[j], row2(b_c_dw[j]), row2(c_norm_g[j]),
                       row2(c_norm_b[j]), wc_out, gffn, wup, wdn)
            hp, new_cp = _conv_ffn_prompt(hp, weights, j, layer, batch, seq, tm)
            hs, new_cs = _conv_ffn_sample(hs, weights, j, layer, state_t)
            cc_p.append(new_cp)
            cc_s.append(new_cs)

    prompt_buf = min(BRANCHES[-1][0], seq)
    ak_p, av_p = [a.reshape(n_ab, batch, N_HEADS, HEAD_DIM, seq).transpose(0, 1, 4, 2, 3)[:, :, seq - prompt_buf:]
                  for a in kv_t]
    return (hp.reshape(batch, seq, D_MODEL), hs.reshape(n_dec, dec_seq, D_MODEL),
            ak_p, av_p, jnp.stack(ak_s), jnp.stack(av_s),
            jnp.stack(bv_p), jnp.stack(bv_s), jnp.stack(cc_p), jnp.stack(cc_s).transpose(0, 2, 1, 3))
```

```python
import functools

import jax
import jax.numpy as jnp
from jax import lax
from jax.experimental import pallas as pl
from jax.experimental.pallas import tpu as pltpu

F32 = jnp.float32
BF16 = jnp.bfloat16

D_MODEL = 1024
N_HEADS = 8
HEAD_DIM = 64
A_WIDTH = N_HEADS * HEAD_DIM
B_WIDTH = 512
N_GROUPS = 8
CHUNK = 128
Q_BLOCK = 128
BRANCHES = ((128, 1), (512, 4), (2048, 16))
ROPE_THETA = 10000.0
CONV_WIDTH = 31
PAST_LEN = 8192
D_FF = 4 * D_MODEL
EPS = 1e-6
IN_AB = 3 * A_WIDTH + 2 * B_WIDTH

LANES = 128
SUBLANES = 8
HALO = 32
CONV_ROWS = 64
FF_CHUNK = 1024
NEG = -1e30
LOG2_E = 1.4426950408889634
QK_SCALE = HEAD_DIM ** -0.5 * LOG2_E
ATTN_GROUP = 16
VMEM_LIMIT = 56 * 1024 * 1024


def _rmsnorm(x, g):
    ms = jnp.mean(x * x, axis=-1, keepdims=True)
    return x * lax.rsqrt(ms + EPS) * g


def _layernorm(x, g, b):
    mu = jnp.mean(x, axis=-1, keepdims=True)
    xc = x - mu
    var = jnp.mean(xc * xc, axis=-1, keepdims=True)
    return xc * lax.rsqrt(var + EPS) * g + b


def _dot(a, b):
    return jnp.dot(a, b, preferred_element_type=F32)


def _ffn(h1, g, wup_ref, wdn_ref):
    n = _rmsnorm(h1, g).astype(BF16)
    acc = h1
    for c in range(D_FF // FF_CHUNK):
        a = _dot(n, wup_ref[:, c * FF_CHUNK:(c + 1) * FF_CHUNK])
        a = jnp.maximum(a, 0.0)
        a = (a * a).astype(BF16)
        acc = acc + _dot(a, wdn_ref[c * FF_CHUNK:(c + 1) * FF_CHUNK, :])
    return acc


def _ab_in_kernel(h_ref, g_ref, w_ref, cos_ref, sin_ref, qg_ref, kg_ref, hm_ref, vbg_ref, vbb_ref, *refs,
                  n_prev, transposed):
    prev_refs, refs = refs[:2 if n_prev else 0], refs[2 if n_prev else 0:]
    q_ref, k_ref, v_ref, u_ref, vb_ref = refs[:5]
    kt_ref, vt_ref = refs[5:] if transposed else (None, None)
    if n_prev:
        kt_ref[0:n_prev] = prev_refs[0][...]
        vt_ref[0:n_prev] = prev_refs[1][...]
    n = _rmsnorm(h_ref[...], g_ref[...]).astype(BF16)
    cos = cos_ref[...]
    sin = sin_ref[...]
    hm = hm_ref[...]
    lane = lax.broadcasted_iota(jnp.int32, (1, LANES), 1)
    first_half = (lane & (HEAD_DIM // 2)) == 0

    def head_norm_rope(z, gain_ref, out_ref, scale, t_ref):
        for c in range(A_WIDTH // LANES):
            sl = slice(c * LANES, (c + 1) * LANES)
            zc = z[:, sl]
            sq = zc * zc
            hi = sq.astype(BF16)
            lo = (sq - hi.astype(F32)).astype(BF16)
            ms = _dot(hi, hm) + _dot(lo, hm)
            y = zc * lax.rsqrt(ms + EPS) * gain_ref[:, sl]
            partner = jnp.where(first_half, pltpu.roll(y, LANES - HEAD_DIM // 2, 1),
                                pltpu.roll(y, HEAD_DIM // 2, 1))
            out = y * cos + partner * sin
            if scale != 1.0:
                out = out * scale
            out_ref[:, sl] = out
            if t_ref is not None:
                t_ref[n_prev, sl, :] = out.T

    head_norm_rope(_dot(n, w_ref[:, 0:A_WIDTH]), qg_ref, q_ref, QK_SCALE, None)
    head_norm_rope(_dot(n, w_ref[:, A_WIDTH:2 * A_WIDTH]), kg_ref, k_ref, 1.0, kt_ref)
    v = _dot(n, w_ref[:, 2 * A_WIDTH:3 * A_WIDTH])
    v_ref[...] = v
    if transposed:
        for c in range(A_WIDTH // LANES):
            sl = slice(c * LANES, (c + 1) * LANES)
            vt_ref[n_prev, sl, :] = v[:, sl].T
    u_ref[...] = jax.nn.gelu(_dot(n, w_ref[:, 3 * A_WIDTH:3 * A_WIDTH + B_WIDTH]))
    vb = jax.nn.gelu(_dot(n, w_ref[:, 3 * A_WIDTH + B_WIDTH:IN_AB]))
    vb_ref[...] = _layernorm(vb, vbg_ref[...], vbb_ref[...])


def _const_spec(shape):
    nd = len(shape)
    return pl.BlockSpec(shape, lambda *_: (0,) * nd, pipeline_mode=pl.Buffered(1))


def _layer_spec(shape, layer):
    nd = len(shape)
    return pl.BlockSpec((None, *shape), lambda *_: (layer,) + (0,) * nd, pipeline_mode=pl.Buffered(1))


def _with_cast_duty(body, n_in, n_out):
    def kernel(*refs):
        ins, srcs = refs[:n_in], refs[n_in:n_in + 2]
        outs, dsts = refs[n_in + 2:n_in + 2 + n_out], refs[n_in + 2 + n_out:n_in + 4 + n_out]
        for src, dst in zip(srcs, dsts):
            dst[...] = src[...].astype(BF16)
        body(*ins, *outs, *refs[n_in + 4 + n_out:])
    return kernel


def _cast_duty_specs(stacks, layer, n_steps, step_of):
    in_specs, out_specs, out_shape = [], [], []
    for w in stacks:
        _, rows, cols = w.shape
        per_step = rows // n_steps
        assert per_step * n_steps == rows and per_step % 16 == 0
        in_specs.append(pl.BlockSpec((None, per_step, cols), lambda *ids: (layer, step_of(*ids), 0)))
        out_specs.append(pl.BlockSpec((per_step, cols), lambda *ids: (step_of(*ids), 0)))
        out_shape.append(jax.ShapeDtypeStruct((rows, cols), BF16))
    return in_specs, out_specs, out_shape


def _ab_in(h, g, w_bf, j, cos_t, sin_t, qg, kg, hm, vbg, vbb, tm, seq=None, prev_t=(), cast=None):
    t = h.shape[0]
    n_pos_blocks = cos_t.shape[0] // tm
    row = lambda i: (i, 0)
    tab = lambda i: (i % n_pos_blocks, 0)
    out = jax.ShapeDtypeStruct((t, A_WIDTH), F32)
    in_specs = [
        pl.BlockSpec((tm, D_MODEL), row), _const_spec((1, D_MODEL)), _layer_spec((D_MODEL, IN_AB), j),
        pl.BlockSpec((tm, LANES), tab), pl.BlockSpec((tm, LANES), tab),
        _const_spec((1, A_WIDTH)), _const_spec((1, A_WIDTH)), _const_spec((LANES, LANES)),
        _const_spec((1, B_WIDTH)), _const_spec((1, B_WIDTH)),
    ]
    out_specs = [pl.BlockSpec((tm, A_WIDTH), row)] * 5
    out_shape = [out] * 5
    n_prev = prev_t[0].shape[0] if prev_t else 0
    if seq is not None:
        tiles_per_seq = seq // tm
        t_map = lambda i: (0, i // tiles_per_seq, 0, i % tiles_per_seq)
        if prev_t:
            in_specs += [pl.BlockSpec((n_prev, None, A_WIDTH, tm), t_map)] * 2
        out_specs += [pl.BlockSpec((n_prev + 1, None, A_WIDTH, tm), t_map)] * 2
        out_shape += [jax.ShapeDtypeStruct((n_prev + 1, t // seq, A_WIDTH, seq), F32)] * 2
    body = functools.partial(_ab_in_kernel, n_prev=n_prev, transposed=seq is not None)
    operands = [h, g, w_bf, cos_t, sin_t, qg, kg, hm, vbg, vbb, *prev_t]
    if cast is not None:
        stacks, cast_layer = cast
        body = _with_cast_duty(body, len(in_specs), len(out_specs))
        c_in, c_out, c_shape = _cast_duty_specs(stacks, cast_layer, t // tm, lambda i: i)
        in_specs, out_specs, out_shape = in_specs + c_in, out_specs + c_out, out_shape + c_shape
        operands += list(stacks)
    return pl.pallas_call(
        body,
        grid=(t // tm,),
        in_specs=in_specs,
        out_specs=out_specs,
        out_shape=out_shape,
        compiler_params=pltpu.CompilerParams(dimension_semantics=("arbitrary",), vmem_limit_bytes=VMEM_LIMIT),
        name="ab_in",
    )(*operands)


def _attn_prompt_kernel(q_ref, k_ref, v_ref, o_ref, ob_scr, lse_scr, *, seq):
    lane = lax.broadcasted_iota(jnp.int32, (1, LANES), 1)
    head0 = lane < HEAD_DIM
    ri = lax.broadcasted_iota(jnp.int32, (2 * Q_BLOCK, 2 * Q_BLOCK), 0) & (Q_BLOCK - 1)
    ci = lax.broadcasted_iota(jnp.int32, (2 * Q_BLOCK, 2 * Q_BLOCK), 1)
    band_two = jnp.where(ci < Q_BLOCK, ci - ri, ri + Q_BLOCK - ci) >= 0
    band_one = (lax.broadcasted_iota(jnp.int32, (2 * Q_BLOCK, Q_BLOCK), 1)
                <= lax.broadcasted_iota(jnp.int32, (2 * Q_BLOCK, Q_BLOCK), 0) & (Q_BLOCK - 1))

    def rows(start, dil):
        if dil == 1:
            return pl.ds(start if isinstance(start, int) else pl.multiple_of(start, Q_BLOCK), Q_BLOCK)
        return pl.ds(start, Q_BLOCK, stride=dil)

    def load(dil, start, with_prev):
        cur = rows(start, dil)
        q = q_ref[0, cur, :]
        q2 = jnp.concatenate([jnp.where(head0, q, 0.0), jnp.where(head0, 0.0, q)], axis=0).astype(BF16)
        k = k_ref[0, cur, :].astype(BF16)
        v = v_ref[0, cur, :].astype(BF16)
        if with_prev:
            prev = rows(start - dil * Q_BLOCK, dil)
            k = jnp.concatenate([k_ref[0, prev, :].astype(BF16), k], axis=0)
            v = jnp.concatenate([v_ref[0, prev, :].astype(BF16), v], axis=0)
        return q2, k, v

    def attend(q2, k, v, with_prev):
        n_keys = k.shape[0]
        s = lax.dot_general(q2, k, (((1,), (1,)), ((), ())), preferred_element_type=F32)
        s = jnp.where(band_two if with_prev else band_one, s, NEG)
        m = jnp.max(s, axis=-1, keepdims=True)
        p = jnp.exp2(s - m).astype(BF16)
        r = _dot(p, jnp.concatenate([v, jnp.ones((n_keys, LANES), BF16)], axis=1))
        den = r[:, LANES:]
        o = r[:, :LANES] * (1.0 / den)
        lse = m + jnp.log2(den)
        return jnp.where(head0, o[:Q_BLOCK], o[Q_BLOCK:]), jnp.where(head0, lse[:Q_BLOCK], lse[Q_BLOCK:])

    def blocks(branch, dil, starts, with_prev):
        loaded = [load(dil, st, with_prev) for st in starts]
        results = [attend(*ld, with_prev) for ld in loaded]
        for st, (o, lse) in zip(starts, results):
            ob_scr[branch, rows(st, dil), :] = o
            lse_scr[branch, rows(st, dil), :] = lse

    def grouped_loop(total, body):
        n_full = total // ATTN_GROUP

        def loop_body(it, carry):
            body([it * ATTN_GROUP + g for g in range(ATTN_GROUP)])
            return carry

        if n_full:
            lax.fori_loop(0, n_full, loop_body, 0)
        if total % ATTN_GROUP:
            body(list(range(n_full * ATTN_GROUP, total)))

    for branch, (window, dil) in enumerate(BRANCHES):
        assert window // dil == Q_BLOCK
        sub_len = seq // dil
        n_blocks = sub_len // Q_BLOCK

        def first_body(items, branch=branch, dil=dil):
            blocks(branch, dil, items, False)

        grouped_loop(dil, first_body)
        if n_blocks > 1:
            def rest_body(items, branch=branch, dil=dil, n_blocks=n_blocks):
                starts = [it // (n_blocks - 1) + dil * Q_BLOCK * (it % (n_blocks - 1) + 1) for it in items]
                blocks(branch, dil, starts, True)

            grouped_loop(dil * (n_blocks - 1), rest_body)

    rows_per_step = 256

    def merge_body(it, carry):
        sl = pl.ds(pl.multiple_of(it * rows_per_step, rows_per_step), rows_per_step)
        lses = [lse_scr[b, sl, :] for b in range(len(BRANCHES))]
        m = functools.reduce(jnp.maximum, lses)
        ws = [jnp.exp2(l - m) for l in lses]
        num = sum(w * ob_scr[b, sl, :] for b, w in enumerate(ws))
        o_ref[0, sl, :] = (num * (1.0 / sum(ws))).astype(o_ref.dtype)
        return carry

    lax.fori_loop(0, seq // rows_per_step, merge_body, 0)


def _attn_prompt(q, k, v):
    b, seq, _ = q.shape
    spec = pl.BlockSpec((1, seq, LANES), lambda i, j: (i, 0, j))
    return pl.pallas_call(
        functools.partial(_attn_prompt_kernel, seq=seq),
        grid=(b, A_WIDTH // LANES),
        in_specs=[spec, spec, spec],
        out_specs=spec,
        out_shape=jax.ShapeDtypeStruct((b, seq, A_WIDTH), BF16),
        scratch_shapes=[pltpu.VMEM((len(BRANCHES), seq, LANES), F32),
                        pltpu.VMEM((len(BRANCHES), seq, LANES), F32)],
        compiler_params=pltpu.CompilerParams(dimension_semantics=("arbitrary", "arbitrary"),
                                             vmem_limit_bytes=VMEM_LIMIT),
        name="attn_prompt",
    )(q, k, v)


def _attn_sample_kernel(q_ref, kn_ref, vn_ref, kt_ref, vt_ref, o_ref, s_scr, sn_scr, p_scr, *, win_buf):
    back = win_buf - lax.broadcasted_iota(jnp.int32, (1, win_buf), 1)
    masks = [jnp.where((back & (dil - 1)) == 0, back, window + 1) <= window for window, dil in BRANCHES]
    for h in range(N_HEADS):
        q = q_ref[h]
        s_scr[h:h + 1, :] = jnp.sum(kt_ref[h] * q, axis=0, keepdims=True)
        sn_scr[h:h + 1, :] = jnp.broadcast_to(jnp.sum(q * kn_ref[h], axis=0, keepdims=True), (1, LANES))
    s = s_scr[...]
    s_new = sn_scr[:, 0:1]
    ps, pns, lses = [], [], []
    for mask in masks:
        sb = jnp.where(mask, s, NEG)
        m = jnp.maximum(jnp.max(sb, axis=1, keepdims=True), s_new)
        p = jnp.exp2(sb - m)
        pn = jnp.exp2(s_new - m)
        den = jnp.sum(p, axis=1, keepdims=True) + pn
        inv = 1.0 / den
        ps.append(p * inv)
        pns.append(pn * inv)
        lses.append(m + jnp.log2(den))
    mm = functools.reduce(jnp.maximum, lses)
    ws = [jnp.exp2(l - mm) for l in lses]
    inv = 1.0 / sum(ws)
    p_scr[...] = sum(w * p for w, p in zip(ws, ps)) * inv
    pn_all = sum(w * pn for w, pn in zip(ws, pns)) * inv
    for h in range(N_HEADS):
        o_ref[h] = jnp.sum(vt_ref[h] * p_scr[h:h + 1, :], axis=1, keepdims=True) + pn_all[h:h + 1] * vn_ref[h]


def _attn_sample(q, k_new, v_new, cache_kt, cache_vt, layer):
    n, win_buf = cache_kt.shape[1], cache_kt.shape[-1]
    for window, dil in BRANCHES:
        assert window <= win_buf <= PAST_LEN and dil & (dil - 1) == 0
    col = pl.BlockSpec((None, N_HEADS, HEAD_DIM, 1), lambda i: (i, 0, 0, 0))
    cache = pl.BlockSpec((None, None, N_HEADS, HEAD_DIM, win_buf), lambda i: (layer, i, 0, 0, 0))
    as_cols = lambda a: a.reshape(n, N_HEADS, HEAD_DIM, 1)
    return pl.pallas_call(
        functools.partial(_attn_sample_kernel, win_buf=win_buf),
        grid=(n,),
        in_specs=[col, col, col, cache, cache],
        out_specs=col,
        out_shape=jax.ShapeDtypeStruct((n, N_HEADS, HEAD_DIM, 1), F32),
        scratch_shapes=[pltpu.VMEM((N_HEADS, win_buf), F32), pltpu.VMEM((N_HEADS, LANES), F32),
                        pltpu.VMEM((N_HEADS, win_buf), F32)],
        compiler_params=pltpu.CompilerParams(dimension_semantics=("arbitrary",), vmem_limit_bytes=VMEM_LIMIT),
        name="attn_sample",
    )(as_cols(q), as_cols(k_new), as_cols(v_new), cache_kt, cache_vt).reshape(n, A_WIDTH)


def _mix_ffn_prompt_kernel(h_ref, att_ref, u_ref, vb_ref, wsp_ref, bsp_ref, wout_ref, gffn_ref, wup_ref, wdn_ref,
                           o_ref, gate_scr, *, tm):
    lane = lax.broadcasted_iota(jnp.int32, (1, LANES), 1)
    lo = lane < B_WIDTH // N_GROUPS
    ri = lax.broadcasted_iota(jnp.int32, (CHUNK, 2 * CHUNK), 0)
    ci = lax.broadcasted_iota(jnp.int32, (CHUNK, 2 * CHUNK), 1)
    causal = (ci & (CHUNK - 1)) <= ri
    n_pairs = B_WIDTH // LANES
    w_pairs = [jnp.where(causal, wsp_ref[p], 0.0).astype(BF16) for p in range(n_pairs)]
    for c in range(tm // CHUNK):
        rows = slice(c * CHUNK, (c + 1) * CHUNK)
        for p in range(n_pairs):
            sl = slice(p * LANES, (p + 1) * LANES)
            slab = vb_ref[rows, sl]
            rhs = jnp.concatenate([jnp.where(lo, slab, 0.0), jnp.where(lo, 0.0, slab)], axis=0).astype(BF16)
            mixed = _dot(w_pairs[p], rhs) + bsp_ref[:, sl]
            gate_scr[rows, sl] = (u_ref[rows, sl] * mixed).astype(BF16)
    mix = _dot(att_ref[...], wout_ref[0:A_WIDTH, :]) + _dot(gate_scr[...], wout_ref[A_WIDTH:, :])
    h1 = h_ref[...] + mix
    o_ref[...] = _ffn(h1, gffn_ref[...], wup_ref, wdn_ref)


def _mix_ffn_sample_kernel(h_ref, att_ref, u_ref, vb_ref, w0_ref, b0_ref, wout_ref, gffn_ref, wup_ref, wdn_ref,
                           o_ref):
    gate = (u_ref[...] * (vb_ref[...] * w0_ref[...] + b0_ref[...])).astype(BF16)
    mix = _dot(att_ref[...].astype(BF16), wout_ref[0:A_WIDTH, :]) + _dot(gate, wout_ref[A_WIDTH:, :])
    h1 = h_ref[...] + mix
    o_ref[...] = _ffn(h1, gffn_ref[...], wup_ref, wdn_ref)


def _mix_ffn(h, att, u, vb, gate_w, gate_b, wout_bf, j, gffn, wup_bf, wdn_bf, tm, sample, cast=None):
    t = h.shape[0]
    row = lambda i: (i, 0)
    if sample:
        body = _mix_ffn_sample_kernel
        scratch = []
    else:
        body = functools.partial(_mix_ffn_prompt_kernel, tm=tm)
        scratch = [pltpu.VMEM((tm, B_WIDTH), BF16)]
    in_specs = [
        pl.BlockSpec((tm, D_MODEL), row), pl.BlockSpec((tm, A_WIDTH), row),
        pl.BlockSpec((tm, B_WIDTH), row), pl.BlockSpec((tm, B_WIDTH), row),
        _const_spec(gate_w.shape), _const_spec(gate_b.shape),
        _layer_spec((A_WIDTH + B_WIDTH, D_MODEL), j), _const_spec((1, D_MODEL)),
        _const_spec((D_MODEL, D_FF)), _const_spec((D_FF, D_MODEL)),
    ]
    out_specs = [pl.BlockSpec((tm, D_MODEL), row)]
    out_shape = [jax.ShapeDtypeStruct((t, D_MODEL), F32)]
    operands = [h, att, u, vb, gate_w, gate_b, wout_bf, gffn, wup_bf, wdn_bf]
    if cast is not None:
        stacks, cast_layer = cast
        body = _with_cast_duty(body, len(in_specs), len(out_specs))
        c_in, c_out, c_shape = _cast_duty_specs(stacks, cast_layer, t // tm, lambda i: i)
        in_specs, out_specs, out_shape = in_specs + c_in, out_specs + c_out, out_shape + c_shape
        operands += list(stacks)
    return pl.pallas_call(
        body,
        grid=(t // tm,),
        in_specs=in_specs,
        out_specs=out_specs,
        out_shape=out_shape,
        scratch_shapes=scratch,
        compiler_params=pltpu.CompilerParams(dimension_semantics=("arbitrary",), vmem_limit_bytes=VMEM_LIMIT),
        name="mix_ffn_sample" if sample else "mix_ffn",
    )(*operands)


def _glu_in(h, gmix, win_ref):
    n = _rmsnorm(h, gmix).astype(BF16)
    a = _dot(n, win_ref[:, 0:D_MODEL])
    gate = _dot(n, win_ref[:, D_MODEL:])
    return a * jax.nn.sigmoid(gate)


def _conv_tail(h, y, cg_ref, cb_ref, wout_ref, gffn_ref, wup_ref, wdn_ref):
    z = _layernorm(y, cg_ref[...], cb_ref[...])
    z = (z * jax.nn.sigmoid(z)).astype(BF16)
    h1 = h + _dot(z, wout_ref[...])
    return _ffn(h1, gffn_ref[...], wup_ref, wdn_ref)


def _conv_ffn_prompt_kernel(h_ref, gmix_ref, win_ref, wdw_ref, bdw_ref, cg_ref, cb_ref, wout_ref, gffn_ref,
                            wup_ref, wdn_ref, o_ref, cc_ref, xbuf, xs_scr, y_scr, h1_scr, *, tm, tiles_per_seq):
    g = pl.program_id(0)
    keep = CONV_WIDTH - 1
    n_slabs = D_MODEL // LANES
    n_chunks = D_FF // FF_CHUNK

    @pl.when(g == 0)
    def _():
        h1_scr[...] = jnp.zeros(h1_scr.shape, F32)
        xbuf[0:HALO, :] = jnp.zeros((HALO, D_MODEL), F32)

    s = jnp.minimum(g, pl.num_programs(0) - 2) % tiles_per_seq
    xbuf[0:HALO, :] = jnp.where(s == 0, 0.0, xbuf[0:HALO, :])
    h = h_ref[...]
    xbuf[HALO:HALO + tm, :] = _glu_in(h, gmix_ref[...], win_ref)

    shift_rows = tm + HALO - SUBLANES
    for b in range(1, SUBLANES):
        xs_scr[b - 1] = xbuf[b:b + shift_rows, :]

    def conv_slab(col, anchor):
        cs = slice(col * LANES, (col + 1) * LANES)
        taps = [wdw_ref[w:w + 1, cs] for w in range(CONV_WIDTH)]
        bias = jnp.broadcast_to(bdw_ref[:, cs], (CONV_ROWS, LANES))
        for r0 in range(0, tm, CONV_ROWS):
            acc = jnp.where(g >= 0, bias, anchor)
            for w in range(CONV_WIDTH):
                a, b = divmod(HALO - keep + w, SUBLANES)
                rows = slice(r0 + SUBLANES * a, r0 + SUBLANES * a + CONV_ROWS)
                x = xbuf[rows, cs] if b == 0 else xs_scr[b - 1, rows, cs]
                acc = acc + x * jnp.where(g >= 0, taps[w], anchor[0:1])
            y_scr[r0:r0 + CONV_ROWS, cs] = acc
            anchor = acc
        return anchor

    h1 = h1_scr[(g + 1) % 2]
    n = _rmsnorm(h1, gffn_ref[...]).astype(BF16)
    acc = h1
    slabs_per_chunk = n_slabs // n_chunks
    for c in range(n_chunks):
        a = jnp.maximum(_dot(n, wup_ref[:, c * FF_CHUNK:(c + 1) * FF_CHUNK]), 0.0)
        anchor = a[0:CONV_ROWS, 0:LANES]
        for i in range(slabs_per_chunk):
            anchor = conv_slab(c * slabs_per_chunk + i, anchor)
        acc = acc + _dot((a * a).astype(BF16), wdn_ref[c * FF_CHUNK:(c + 1) * FF_CHUNK, :])
    o_ref[...] = acc

    cc_ref[0] = xbuf[HALO + tm - keep:HALO + tm, :]
    xbuf[0:HALO, :] = xbuf[tm:tm + HALO, :]
    z = _layernorm(y_scr[...], cg_ref[...], cb_ref[...])
    z = (z * jax.nn.sigmoid(z)).astype(BF16)
    h1_scr[g % 2] = h + _dot(z, wout_ref[...])


def _conv_ffn_sample_kernel(h_ref, gmix_ref, win_ref, wdw_ref, bdw_ref, cg_ref, cb_ref, wout_ref, gffn_ref,
                            wup_ref, wdn_ref, st_ref, o_ref, cc_ref):
    keep = CONV_WIDTH - 1
    h = h_ref[...]
    x = _glu_in(h, gmix_ref[...], win_ref)
    y = x * wdw_ref[keep:keep + 1, :] + bdw_ref[...]
    for w in range(keep):
        y = y + st_ref[w] * wdw_ref[w:w + 1, :]
    for w in range(keep - 1):
        cc_ref[w] = st_ref[w + 1]
    cc_ref[keep - 1] = x
    o_ref[...] = _conv_tail(h, y, cg_ref, cb_ref, wout_ref, gffn_ref, wup_ref, wdn_ref)


def _conv_weight_specs(j):
    return [
        _const_spec((1, D_MODEL)), _layer_spec((D_MODEL, 2 * D_MODEL), j), _const_spec((CONV_WIDTH, D_MODEL)),
        _const_spec((1, D_MODEL)), _const_spec((1, D_MODEL)), _const_spec((1, D_MODEL)),
        _layer_spec((D_MODEL, D_MODEL), j), _const_spec((1, D_MODEL)),
        _const_spec((D_MODEL, D_FF)), _const_spec((D_FF, D_MODEL)),
    ]


def _conv_ffn_prompt(h, weights, j, batch, seq, tm, cast=None):
    keep = CONV_WIDTH - 1
    n_s = seq // tm
    n_tiles = batch * n_s
    conv_tile = lambda g: (jnp.minimum(g, n_tiles - 1), 0)
    ffn_tile = lambda g: (jnp.maximum(g - 1, 0), 0)
    body = functools.partial(_conv_ffn_prompt_kernel, tm=tm, tiles_per_seq=n_s)
    in_specs = [pl.BlockSpec((tm, D_MODEL), conv_tile)] + _conv_weight_specs(j)
    out_specs = [pl.BlockSpec((tm, D_MODEL), ffn_tile),
                 pl.BlockSpec((1, keep, D_MODEL), lambda g: (jnp.minimum(g, n_tiles - 1) // n_s, 0, 0))]
    out_shape = [jax.ShapeDtypeStruct((batch * seq, D_MODEL), F32),
                 jax.ShapeDtypeStruct((batch, keep, D_MODEL), F32)]
    operands = [h, *weights]
    if cast is not None:
        stacks, cast_layer = cast
        body = _with_cast_duty(body, len(in_specs), len(out_specs))
        c_in, c_out, c_shape = _cast_duty_specs(stacks, cast_layer, n_tiles, lambda g: jnp.minimum(g, n_tiles - 1))
        in_specs, out_specs, out_shape = in_specs + c_in, out_specs + c_out, out_shape + c_shape
        operands += list(stacks)
    return pl.pallas_call(
        body,
        grid=(n_tiles + 1,),
        in_specs=in_specs,
        out_specs=out_specs,
        out_shape=out_shape,
        scratch_shapes=[pltpu.VMEM((HALO + tm, D_MODEL), F32),
                        pltpu.VMEM((SUBLANES - 1, HALO + tm - SUBLANES, D_MODEL), F32),
                        pltpu.VMEM((tm, D_MODEL), F32),
                        pltpu.VMEM((2, tm, D_MODEL), F32)],
        compiler_params=pltpu.CompilerParams(dimension_semantics=("arbitrary",), vmem_limit_bytes=VMEM_LIMIT),
        name="conv_ffn",
    )(*operands)


def _conv_ffn_sample(h, weights, j, state_t):
    n = h.shape[0]
    keep = CONV_WIDTH - 1
    return pl.pallas_call(
        _conv_ffn_sample_kernel,
        grid=(1,),
        in_specs=[_const_spec((n, D_MODEL))] + _conv_weight_specs(j)
        + [pl.BlockSpec((None, keep, n, D_MODEL), lambda i: (j, 0, 0, 0))],
        out_specs=[_const_spec((n, D_MODEL)), _const_spec((keep, n, D_MODEL))],
        out_shape=[jax.ShapeDtypeStruct((n, D_MODEL), F32), jax.ShapeDtypeStruct((keep, n, D_MODEL), F32)],
        compiler_params=pltpu.CompilerParams(dimension_semantics=("arbitrary",), vmem_limit_bytes=VMEM_LIMIT),
        name="conv_ffn_sample",
    )(h, *weights, state_t)


def _rope_tables(pos):
    inv = ROPE_THETA ** (-jnp.arange(0, HEAD_DIM, 2, dtype=F32) / HEAD_DIM)
    ang = pos.astype(F32)[:, None] * inv[None, :]
    cos, sin = jnp.cos(ang), jnp.sin(ang)
    reps = LANES // HEAD_DIM
    return jnp.tile(jnp.concatenate([cos, cos], axis=-1), (1, reps)), jnp.tile(jnp.concatenate([-sin, sin], axis=-1), (1, reps))


def kernel(x_prompt, x_sample, cache_a_k, cache_a_v, state_c_conv, norm_mix_g, norm_ffn_g, w_ffn_up, w_ffn_down, w_in_ab, q_norm_g, k_norm_g, vb_norm_g, vb_norm_b, w_spatial, b_spatial, w_out_ab, w_c_in, w_c_dw, b_c_dw, c_norm_g, c_norm_b, w_c_out):
    batch, seq, _ = x_prompt.shape
    n_dec, dec_seq, _ = x_sample.shape
    depth = norm_mix_g.shape[0]
    n_ab, _, win_buf = cache_a_k.shape[:3]
    past_len = PAST_LEN
    assert dec_seq == 1 and seq % (BRANCHES[-1][1] * Q_BLOCK) == 0 and seq % CHUNK == 0
    tm = 256

    hp = x_prompt.reshape(batch * seq, D_MODEL)
    hs = x_sample.reshape(n_dec * dec_seq, D_MODEL)
    cos_p, sin_p = _rope_tables(jnp.arange(seq, dtype=jnp.int32))
    cos_s, sin_s = _rope_tables(jnp.full((n_dec,), past_len, dtype=jnp.int32))
    head_id = jnp.arange(LANES) // HEAD_DIM
    head_mean = jnp.where(head_id[:, None] == head_id[None, :], 1.0 / HEAD_DIM, 0.0).astype(BF16)
    cache_kt = cache_a_k.transpose(0, 1, 3, 4, 2)
    cache_vt = cache_a_v.transpose(0, 1, 3, 4, 2)
    state_t = state_c_conv.transpose(0, 2, 1, 3)
    group_dim = B_WIDTH // N_GROUPS
    row2 = lambda a: a.reshape(1, -1)

    w_in, wout = w_in_ab.astype(BF16), w_out_ab.astype(BF16)
    wc_in, wc_out = w_c_in.astype(BF16), w_c_out.astype(BF16)
    ffn_f32 = (w_ffn_up, w_ffn_down)
    ffn_bf = {}
    next_cast = lambda layer: (ffn_f32, layer + 1) if layer + 1 < depth else None

    ak_s, av_s, bv_p, bv_s, cc_p, cc_s = [], [], [], [], [], []
    kv_t = ()
    for layer in range(depth):
        j = layer // 2
        gffn = row2(norm_ffn_g[layer])
        gmix = row2(norm_mix_g[layer])
        if layer % 2 == 0:
            qg = row2(jnp.tile(q_norm_g[j], N_HEADS))
            kg = row2(jnp.tile(k_norm_g[j], N_HEADS))
            vbg, vbb = row2(vb_norm_g[j]), row2(vb_norm_b[j])
            qp, kp, vp, up, vbp, *rest = _ab_in(hp, gmix, w_in, j, cos_p, sin_p, qg, kg, head_mean, vbg, vbb, tm,
                                                seq=seq, prev_t=tuple(kv_t),
                                                cast=None if layer in ffn_bf else (ffn_f32, layer))
            kv_t = rest[:2]
            if layer not in ffn_bf:
                ffn_bf[layer] = tuple(rest[2:])
            wup, wdn = ffn_bf[layer]
            qs, ks, vs, us, vbs = _ab_in(hs, gmix, w_in, j, cos_s, sin_s, qg, kg, head_mean, vbg, vbb, n_dec)
            shp = (batch, seq, A_WIDTH)
            att_p = _attn_prompt(qp.reshape(shp), kp.reshape(shp), vp.reshape(shp)).reshape(batch * seq, A_WIDTH)
            att_s = _attn_sample(qs, ks, vs, cache_kt, cache_vt, j)
            w_pairs = w_spatial[j].reshape(N_GROUPS // 2, 2, CHUNK, CHUNK).transpose(0, 2, 1, 3).reshape(N_GROUPS // 2, CHUNK, 2 * CHUNK)
            b_rows = jnp.repeat(b_spatial[j].T, group_dim, axis=1)
            w0 = row2(jnp.repeat(w_spatial[j][:, 0, 0], group_dim))
            b0 = row2(jnp.repeat(b_spatial[j][:, 0], group_dim))
            hp, *cast_out = _mix_ffn(hp, att_p, up, vbp, w_pairs, b_rows, wout, j, gffn, wup, wdn, tm, False,
                                     cast=next_cast(layer))
            hs, = _mix_ffn(hs, att_s, us, vbs, w0, b0, wout, j, gffn, wup, wdn, n_dec, True)
            ak_s.append(ks.reshape(n_dec, dec_seq, N_HEADS, HEAD_DIM))
            av_s.append(vs.reshape(n_dec, dec_seq, N_HEADS, HEAD_DIM))
            last_chunk_start = ((seq - 1) // CHUNK) * CHUNK
            bv_p.append(vbp.reshape(batch, seq, B_WIDTH)[:, last_chunk_start:])
            bv_s.append(vbs.reshape(n_dec, dec_seq, B_WIDTH))
        else:
            wup, wdn = ffn_bf[layer]
            weights = (gmix, wc_in, w_c_dw[j], row2(b_c_dw[j]), row2(c_norm_g[j]),
                       row2(c_norm_b[j]), wc_out, gffn, wup, wdn)
            hp, new_cp, *cast_out = _conv_ffn_prompt(hp, weights, j, batch, seq, tm, cast=next_cast(layer))
            hs, new_cs = _conv_ffn_sample(hs, weights, j, state_t)
            cc_p.append(new_cp)
            cc_s.append(new_cs)
        if cast_out:
            ffn_bf[layer + 1] = tuple(cast_out)

    prompt_buf = min(BRANCHES[-1][0], seq)
    ak_p, av_p = [a.reshape(n_ab, batch, N_HEADS, HEAD_DIM, seq).transpose(0, 1, 4, 2, 3)[:, :, seq - prompt_buf:]
                  for a in kv_t]
    return (hp.reshape(batch, seq, D_MODEL), hs.reshape(n_dec, dec_seq, D_MODEL),
            ak_p, av_p, jnp.stack(ak_s), jnp.stack(av_s),
            jnp.stack(bv_p), jnp.stack(bv_s), jnp.stack(cc_p), jnp.stack(cc_s).transpose(0, 2, 1, 3))
```

```python
import functools

import jax
import jax.numpy as jnp
from jax import lax
from jax.experimental import pallas as pl
from jax.experimental.pallas import tpu as pltpu

F32 = jnp.float32
BF16 = jnp.bfloat16

D_MODEL = 1024
N_HEADS = 8
HEAD_DIM = 64
A_WIDTH = N_HEADS * HEAD_DIM
B_WIDTH = 512
N_GROUPS = 8
CHUNK = 128
Q_BLOCK = 128
BRANCHES = ((128, 1), (512, 4), (2048, 16))
ROPE_THETA = 10000.0
CONV_WIDTH = 31
PAST_LEN = 8192
D_FF = 4 * D_MODEL
EPS = 1e-6
IN_AB = 3 * A_WIDTH + 2 * B_WIDTH

LANES = 128
SUBLANES = 8
HALO = 32
CONV_ROWS = 64
FF_CHUNK = 1024
NEG = -1e30
LOG2_E = 1.4426950408889634
QK_SCALE = HEAD_DIM ** -0.5 * LOG2_E
ATTN_GROUP = 16
VMEM_LIMIT = 56 * 1024 * 1024


def _rmsnorm(x, g):
    ms = jnp.mean(x * x, axis=-1, keepdims=True)
    return x * lax.rsqrt(ms + EPS) * g


def _layernorm(x, g, b):
    mu = jnp.mean(x, axis=-1, keepdims=True)
    xc = x - mu
    var = jnp.mean(xc * xc, axis=-1, keepdims=True)
    return xc * lax.rsqrt(var + EPS) * g + b


def _dot(a, b):
    return jnp.dot(a, b, preferred_element_type=F32)


def _ffn(h1, g, wup_ref, wdn_ref):
    n = _rmsnorm(h1, g).astype(BF16)
    acc = h1
    for c in range(D_FF // FF_CHUNK):
        a = _dot(n, wup_ref[:, c * FF_CHUNK:(c + 1) * FF_CHUNK])
        a = jnp.maximum(a, 0.0)
        a = (a * a).astype(BF16)
        acc = acc + _dot(a, wdn_ref[c * FF_CHUNK:(c + 1) * FF_CHUNK, :])
    return acc


def _ab_in_kernel(h_ref, g_ref, w_ref, cos_ref, sin_ref, qg_ref, kg_ref, hm_ref, vbg_ref, vbb_ref, *refs,
                  n_prev, transposed):
    prev_refs, refs = refs[:2 if n_prev else 0], refs[2 if n_prev else 0:]
    q_ref, k_ref, v_ref, u_ref, vb_ref = refs[:5]
    kt_ref, vt_ref = refs[5:] if transposed else (None, None)
    if n_prev:
        kt_ref[0:n_prev] = prev_refs[0][...]
        vt_ref[0:n_prev] = prev_refs[1][...]
    n = _rmsnorm(h_ref[...], g_ref[...]).astype(BF16)
    cos = cos_ref[...]
    sin = sin_ref[...]
    hm = hm_ref[...]
    lane = lax.broadcasted_iota(jnp.int32, (1, LANES), 1)
    first_half = (lane & (HEAD_DIM // 2)) == 0

    def head_norm_rope(z, gain_ref, out_ref, scale, t_ref):
        for c in range(A_WIDTH // LANES):
            sl = slice(c * LANES, (c + 1) * LANES)
            zc = z[:, sl]
            sq = zc * zc
            hi = sq.astype(BF16)
            lo = (sq - hi.astype(F32)).astype(BF16)
            ms = _dot(hi, hm) + _dot(lo, hm)
            y = zc * lax.rsqrt(ms + EPS) * gain_ref[:, sl]
            partner = jnp.where(first_half, pltpu.roll(y, LANES - HEAD_DIM // 2, 1),
                                pltpu.roll(y, HEAD_DIM // 2, 1))
            out = y * cos + partner * sin
            if scale != 1.0:
                out = out * scale
            out_ref[:, sl] = out
            if t_ref is not None:
                t_ref[n_prev, sl, :] = out.T

    head_norm_rope(_dot(n, w_ref[:, 0:A_WIDTH]), qg_ref, q_ref, QK_SCALE, None)
    head_norm_rope(_dot(n, w_ref[:, A_WIDTH:2 * A_WIDTH]), kg_ref, k_ref, 1.0, kt_ref)
    v = _dot(n, w_ref[:, 2 * A_WIDTH:3 * A_WIDTH])
    v_ref[...] = v
    if transposed:
        for c in range(A_WIDTH // LANES):
            sl = slice(c * LANES, (c + 1) * LANES)
            vt_ref[n_prev, sl, :] = v[:, sl].T
    u_ref[...] = jax.nn.gelu(_dot(n, w_ref[:, 3 * A_WIDTH:3 * A_WIDTH + B_WIDTH]))
    vb = jax.nn.gelu(_dot(n, w_ref[:, 3 * A_WIDTH + B_WIDTH:IN_AB]))
    vb_ref[...] = _layernorm(vb, vbg_ref[...], vbb_ref[...])


def _const_spec(shape):
    nd = len(shape)
    return pl.BlockSpec(shape, lambda *_: (0,) * nd, pipeline_mode=pl.Buffered(1))


def _layer_spec(shape, layer):
    nd = len(shape)
    return pl.BlockSpec((None, *shape), lambda *_: (layer,) + (0,) * nd, pipeline_mode=pl.Buffered(1))


def _with_cast_duty(body, n_in, n_out):
    def kernel(*refs):
        ins, srcs = refs[:n_in], refs[n_in:n_in + 2]
        outs, dsts = refs[n_in + 2:n_in + 2 + n_out], refs[n_in + 2 + n_out:n_in + 4 + n_out]
        for src, dst in zip(srcs, dsts):
            dst[...] = src[...].astype(BF16)
        body(*ins, *outs, *refs[n_in + 4 + n_out:])
    return kernel


def _cast_duty_specs(stacks, layer, n_steps, step_of):
    in_specs, out_specs, out_shape = [], [], []
    for w in stacks:
        _, rows, cols = w.shape
        per_step = rows // n_steps
        assert per_step * n_steps == rows and per_step % 16 == 0
        in_specs.append(pl.BlockSpec((None, per_step, cols), lambda *ids: (layer, step_of(*ids), 0)))
        out_specs.append(pl.BlockSpec((per_step, cols), lambda *ids: (step_of(*ids), 0)))
        out_shape.append(jax.ShapeDtypeStruct((rows, cols), BF16))
    return in_specs, out_specs, out_shape


def _ab_in(h, g, w_bf, j, cos_t, sin_t, qg, kg, hm, vbg, vbb, tm, seq=None, prev_t=(), cast=None):
    t = h.shape[0]
    n_pos_blocks = cos_t.shape[0] // tm
    row = lambda i: (i, 0)
    tab = lambda i: (i % n_pos_blocks, 0)
    out = jax.ShapeDtypeStruct((t, A_WIDTH), F32)
    in_specs = [
        pl.BlockSpec((tm, D_MODEL), row), _const_spec((1, D_MODEL)), _layer_spec((D_MODEL, IN_AB), j),
        pl.BlockSpec((tm, LANES), tab), pl.BlockSpec((tm, LANES), tab),
        _const_spec((1, A_WIDTH)), _const_spec((1, A_WIDTH)), _const_spec((LANES, LANES)),
        _const_spec((1, B_WIDTH)), _const_spec((1, B_WIDTH)),
    ]
    out_specs = [pl.BlockSpec((tm, A_WIDTH), row)] * 5
    out_shape = [out] * 5
    n_prev = prev_t[0].shape[0] if prev_t else 0
    if seq is not None:
        tiles_per_seq = seq // tm
        t_map = lambda i: (0, i // tiles_per_seq, 0, i % tiles_per_seq)
        if prev_t:
            in_specs += [pl.BlockSpec((n_prev, None, A_WIDTH, tm), t_map)] * 2
        out_specs += [pl.BlockSpec((n_prev + 1, None, A_WIDTH, tm), t_map)] * 2
        out_shape += [jax.ShapeDtypeStruct((n_prev + 1, t // seq, A_WIDTH, seq), F32)] * 2
    body = functools.partial(_ab_in_kernel, n_prev=n_prev, transposed=seq is not None)
    operands = [h, g, w_bf, cos_t, sin_t, qg, kg, hm, vbg, vbb, *prev_t]
    if cast is not None:
        stacks, cast_layer = cast
        body = _with_cast_duty(body, len(in_specs), len(out_specs))
        c_in, c_out, c_shape = _cast_duty_specs(stacks, cast_layer, t // tm, lambda i: i)
        in_specs, out_specs, out_shape = in_specs + c_in, out_specs + c_out, out_shape + c_shape
        operands += list(stacks)
    return pl.pallas_call(
        body,
        grid=(t // tm,),
        in_specs=in_specs,
        out_specs=out_specs,
        out_shape=out_shape,
        compiler_params=pltpu.CompilerParams(dimension_semantics=("arbitrary",), vmem_limit_bytes=VMEM_LIMIT),
        name="ab_in",
    )(*operands)


def _attn_prompt_kernel(q_ref, k_ref, v_ref, o_ref, ob_scr, lse_scr, *, seq):
    lane = lax.broadcasted_iota(jnp.int32, (1, LANES), 1)
    head0 = lane < HEAD_DIM
    ri = lax.broadcasted_iota(jnp.int32, (2 * Q_BLOCK, 2 * Q_BLOCK), 0) & (Q_BLOCK - 1)
    ci = lax.broadcasted_iota(jnp.int32, (2 * Q_BLOCK, 2 * Q_BLOCK), 1)
    band_two = jnp.where(ci < Q_BLOCK, ci - ri, ri + Q_BLOCK - ci) >= 0
    band_one = (lax.broadcasted_iota(jnp.int32, (2 * Q_BLOCK, Q_BLOCK), 1)
                <= lax.broadcasted_iota(jnp.int32, (2 * Q_BLOCK, Q_BLOCK), 0) & (Q_BLOCK - 1))

    def rows(start, dil):
        if dil == 1:
            return pl.ds(start if isinstance(start, int) else pl.multiple_of(start, Q_BLOCK), Q_BLOCK)
        return pl.ds(start, Q_BLOCK, stride=dil)

    def load(dil, start, with_prev):
        cur = rows(start, dil)
        q = q_ref[0, cur, :]
        q2 = jnp.concatenate([jnp.where(head0, q, 0.0), jnp.where(head0, 0.0, q)], axis=0).astype(BF16)
        k = k_ref[0, cur, :].astype(BF16)
        v = v_ref[0, cur, :].astype(BF16)
        if with_prev:
            prev = rows(start - dil * Q_BLOCK, dil)
            k = jnp.concatenate([k_ref[0, prev, :].astype(BF16), k], axis=0)
            v = jnp.concatenate([v_ref[0, prev, :].astype(BF16), v], axis=0)
        return q2, k, v

    def attend(q2, k, v, with_prev):
        n_keys = k.shape[0]
        s = lax.dot_general(q2, k, (((1,), (1,)), ((), ())), preferred_element_type=F32)
        s = jnp.where(band_two if with_prev else band_one, s, NEG)
        m = jnp.max(s, axis=-1, keepdims=True)
        p = jnp.exp2(s - m).astype(BF16)
        r = _dot(p, jnp.concatenate([v, jnp.ones((n_keys, LANES), BF16)], axis=1))
        den = r[:, LANES:]
        o = r[:, :LANES] * (1.0 / den)
        lse = m + jnp.log2(den)
        return jnp.where(head0, o[:Q_BLOCK], o[Q_BLOCK:]), jnp.where(head0, lse[:Q_BLOCK], lse[Q_BLOCK:])

    def blocks(branch, dil, starts, with_prev):
        loaded = [load(dil, st, with_prev) for st in starts]
        results = [attend(*ld, with_prev) for ld in loaded]
        for st, (o, lse) in zip(starts, results):
            ob_scr[branch, rows(st, dil), :] = o
            lse_scr[branch, rows(st, dil), :] = lse

    def grouped_loop(total, body):
        n_full = total // ATTN_GROUP

        def loop_body(it, carry):
            body([it * ATTN_GROUP + g for g in range(ATTN_GROUP)])
            return carry

        if n_full:
            lax.fori_loop(0, n_full, loop_body, 0)
        if total % ATTN_GROUP:
            body(list(range(n_full * ATTN_GROUP, total)))

    for branch, (window, dil) in enumerate(BRANCHES):
        assert window // dil == Q_BLOCK
        sub_len = seq // dil
        n_blocks = sub_len // Q_BLOCK

        def first_body(items, branch=branch, dil=dil):
            blocks(branch, dil, items, False)

        grouped_loop(dil, first_body)
        if n_blocks > 1:
            def rest_body(items, branch=branch, dil=dil, n_blocks=n_blocks):
                starts = [it // (n_blocks - 1) + dil * Q_BLOCK * (it % (n_blocks - 1) + 1) for it in items]
                blocks(branch, dil, starts, True)

            grouped_loop(dil * (n_blocks - 1), rest_body)

    rows_per_step = 256

    def merge_body(it, carry):
        sl = pl.ds(pl.multiple_of(it * rows_per_step, rows_per_step), rows_per_step)
        lses = [lse_scr[b, sl, :] for b in range(len(BRANCHES))]
        m = functools.reduce(jnp.maximum, lses)
        ws = [jnp.exp2(l - m) for l in lses]
        num = sum(w * ob_scr[b, sl, :] for b, w in enumerate(ws))
        o_ref[0, sl, :] = (num * (1.0 / sum(ws))).astype(o_ref.dtype)
        return carry

    lax.fori_loop(0, seq // rows_per_step, merge_body, 0)


def _attn(q, k, v, q_s, k_s, v_s, cache_kt, cache_vt, layer):
    b, seq, _ = q.shape
    n_slabs = A_WIDTH // LANES
    n, win_buf = cache_kt.shape[1], cache_kt.shape[-1]
    assert n == b * n_slabs
    for window, dil in BRANCHES:
        assert window <= win_buf <= PAST_LEN and dil & (dil - 1) == 0
    spec = pl.BlockSpec((1, seq, LANES), lambda i, j: (i, 0, j))
    col = pl.BlockSpec((None, N_HEADS, HEAD_DIM, 1), lambda i, j: (i * n_slabs + j, 0, 0, 0))
    cache = pl.BlockSpec((None, None, N_HEADS, HEAD_DIM, win_buf), lambda i, j: (layer, i * n_slabs + j, 0, 0, 0))
    as_cols = lambda a: a.reshape(n, N_HEADS, HEAD_DIM, 1)
    prompt_scratch = [pltpu.VMEM((len(BRANCHES), seq, LANES), F32), pltpu.VMEM((len(BRANCHES), seq, LANES), F32)]
    sample_scratch = [pltpu.VMEM((N_HEADS, win_buf), F32), pltpu.VMEM((N_HEADS, LANES), F32),
                      pltpu.VMEM((N_HEADS, win_buf), F32)]

    def kernel(*refs):
        prompt_in, sample_in, (o_ref, os_ref), scratch = refs[:3], refs[3:8], refs[8:10], refs[10:]
        _attn_sample_kernel(*sample_in, os_ref, *scratch[len(prompt_scratch):], win_buf=win_buf)
        _attn_prompt_kernel(*prompt_in, o_ref, *scratch[:len(prompt_scratch)], seq=seq)

    att_p, att_s = pl.pallas_call(
        kernel,
        grid=(b, n_slabs),
        in_specs=[spec, spec, spec, col, col, col, cache, cache],
        out_specs=[spec, col],
        out_shape=[jax.ShapeDtypeStruct((b, seq, A_WIDTH), BF16),
                   jax.ShapeDtypeStruct((n, N_HEADS, HEAD_DIM, 1), F32)],
        scratch_shapes=prompt_scratch + sample_scratch,
        compiler_params=pltpu.CompilerParams(dimension_semantics=("arbitrary", "arbitrary"),
                                             vmem_limit_bytes=VMEM_LIMIT),
        name="attn",
    )(q, k, v, as_cols(q_s), as_cols(k_s), as_cols(v_s), cache_kt, cache_vt)
    return att_p, att_s.reshape(n, A_WIDTH)


def _attn_sample_kernel(q_ref, kn_ref, vn_ref, kt_ref, vt_ref, o_ref, s_scr, sn_scr, p_scr, *, win_buf):
    back = win_buf - lax.broadcasted_iota(jnp.int32, (1, win_buf), 1)
    masks = [jnp.where((back & (dil - 1)) == 0, back, window + 1) <= window for window, dil in BRANCHES]
    for h in range(N_HEADS):
        q = q_ref[h]
        s_scr[h:h + 1, :] = jnp.sum(kt_ref[h] * q, axis=0, keepdims=True)
        sn_scr[h:h + 1, :] = jnp.broadcast_to(jnp.sum(q * kn_ref[h], axis=0, keepdims=True), (1, LANES))
    s = s_scr[...]
    s_new = sn_scr[:, 0:1]
    ps, pns, lses = [], [], []
    for mask in masks:
        sb = jnp.where(mask, s, NEG)
        m = jnp.maximum(jnp.max(sb, axis=1, keepdims=True), s_new)
        p = jnp.exp2(sb - m)
        pn = jnp.exp2(s_new - m)
        den = jnp.sum(p, axis=1, keepdims=True) + pn
        inv = 1.0 / den
        ps.append(p * inv)
        pns.append(pn * inv)
        lses.append(m + jnp.log2(den))
    mm = functools.reduce(jnp.maximum, lses)
    ws = [jnp.exp2(l - mm) for l in lses]
    inv = 1.0 / sum(ws)
    p_scr[...] = sum(w * p for w, p in zip(ws, ps)) * inv
    pn_all = sum(w * pn for w, pn in zip(ws, pns)) * inv
    for h in range(N_HEADS):
        o_ref[h] = jnp.sum(vt_ref[h] * p_scr[h:h + 1, :], axis=1, keepdims=True) + pn_all[h:h + 1] * vn_ref[h]


def _mix_ffn_prompt_kernel(h_ref, att_ref, u_ref, vb_ref, wsp_ref, bsp_ref, wout_ref, gffn_ref, wup_ref, wdn_ref,
                           o_ref, gate_scr, *, tm):
    lane = lax.broadcasted_iota(jnp.int32, (1, LANES), 1)
    lo = lane < B_WIDTH // N_GROUPS
    ri = lax.broadcasted_iota(jnp.int32, (CHUNK, 2 * CHUNK), 0)
    ci = lax.broadcasted_iota(jnp.int32, (CHUNK, 2 * CHUNK), 1)
    causal = (ci & (CHUNK - 1)) <= ri
    n_pairs = B_WIDTH // LANES
    w_pairs = [jnp.where(causal, wsp_ref[p], 0.0).astype(BF16) for p in range(n_pairs)]
    for c in range(tm // CHUNK):
        rows = slice(c * CHUNK, (c + 1) * CHUNK)
        for p in range(n_pairs):
            sl = slice(p * LANES, (p + 1) * LANES)
            slab = vb_ref[rows, sl]
            rhs = jnp.concatenate([jnp.where(lo, slab, 0.0), jnp.where(lo, 0.0, slab)], axis=0).astype(BF16)
            mixed = _dot(w_pairs[p], rhs) + bsp_ref[:, sl]
            gate_scr[rows, sl] = (u_ref[rows, sl] * mixed).astype(BF16)
    mix = _dot(att_ref[...], wout_ref[0:A_WIDTH, :]) + _dot(gate_scr[...], wout_ref[A_WIDTH:, :])
    h1 = h_ref[...] + mix
    o_ref[...] = _ffn(h1, gffn_ref[...], wup_ref, wdn_ref)


def _mix_ffn_sample_kernel(h_ref, att_ref, u_ref, vb_ref, w0_ref, b0_ref, wout_ref, gffn_ref, wup_ref, wdn_ref,
                           o_ref):
    gate = (u_ref[...] * (vb_ref[...] * w0_ref[...] + b0_ref[...])).astype(BF16)
    mix = _dot(att_ref[...].astype(BF16), wout_ref[0:A_WIDTH, :]) + _dot(gate, wout_ref[A_WIDTH:, :])
    h1 = h_ref[...] + mix
    o_ref[...] = _ffn(h1, gffn_ref[...], wup_ref, wdn_ref)


def _mix_ffn(h, att, u, vb, gate_w, gate_b, wout_bf, j, gffn, wup_bf, wdn_bf, tm, sample, cast=None):
    t = h.shape[0]
    row = lambda i: (i, 0)
    if sample:
        body = _mix_ffn_sample_kernel
        scratch = []
    else:
        body = functools.partial(_mix_ffn_prompt_kernel, tm=tm)
        scratch = [pltpu.VMEM((tm, B_WIDTH), BF16)]
    in_specs = [
        pl.BlockSpec((tm, D_MODEL), row), pl.BlockSpec((tm, A_WIDTH), row),
        pl.BlockSpec((tm, B_WIDTH), row), pl.BlockSpec((tm, B_WIDTH), row),
        _const_spec(gate_w.shape), _const_spec(gate_b.shape),
        _layer_spec((A_WIDTH + B_WIDTH, D_MODEL), j), _const_spec((1, D_MODEL)),
        _const_spec((D_MODEL, D_FF)), _const_spec((D_FF, D_MODEL)),
    ]
    out_specs = [pl.BlockSpec((tm, D_MODEL), row)]
    out_shape = [jax.ShapeDtypeStruct((t, D_MODEL), F32)]
    operands = [h, att, u, vb, gate_w, gate_b, wout_bf, gffn, wup_bf, wdn_bf]
    if cast is not None:
        stacks, cast_layer = cast
        body = _with_cast_duty(body, len(in_specs), len(out_specs))
        c_in, c_out, c_shape = _cast_duty_specs(stacks, cast_layer, t // tm, lambda i: i)
        in_specs, out_specs, out_shape = in_specs + c_in, out_specs + c_out, out_shape + c_shape
        operands += list(stacks)
    return pl.pallas_call(
        body,
        grid=(t // tm,),
        in_specs=in_specs,
        out_specs=out_specs,
        out_shape=out_shape,
        scratch_shapes=scratch,
        compiler_params=pltpu.CompilerParams(dimension_semantics=("arbitrary",), vmem_limit_bytes=VMEM_LIMIT),
        name="mix_ffn_sample" if sample else "mix_ffn",
    )(*operands)


def _glu_in(h, gmix, win_ref):
    n = _rmsnorm(h, gmix).astype(BF16)
    a = _dot(n, win_ref[:, 0:D_MODEL])
    gate = _dot(n, win_ref[:, D_MODEL:])
    return a * jax.nn.sigmoid(gate)


def _conv_tail(h, y, cg_ref, cb_ref, wout_ref, gffn_ref, wup_ref, wdn_ref):
    z = _layernorm(y, cg_ref[...], cb_ref[...])
    z = (z * jax.nn.sigmoid(z)).astype(BF16)
    h1 = h + _dot(z, wout_ref[...])
    return _ffn(h1, gffn_ref[...], wup_ref, wdn_ref)


def _conv_ffn_prompt_kernel(h_ref, gmix_ref, win_ref, wdw_ref, bdw_ref, cg_ref, cb_ref, wout_ref, gffn_ref,
                            wup_ref, wdn_ref, o_ref, cc_ref, xbuf, xs_scr, y_scr, h1_scr, *, tm, tiles_per_seq):
    g = pl.program_id(0)
    keep = CONV_WIDTH - 1
    n_slabs = D_MODEL // LANES
    n_chunks = D_FF // FF_CHUNK

    @pl.when(g == 0)
    def _():
        h1_scr[...] = jnp.zeros(h1_scr.shape, F32)
        xbuf[0:HALO, :] = jnp.zeros((HALO, D_MODEL), F32)

    s = jnp.minimum(g, pl.num_programs(0) - 2) % tiles_per_seq
    xbuf[0:HALO, :] = jnp.where(s == 0, 0.0, xbuf[0:HALO, :])
    h = h_ref[...]
    xbuf[HALO:HALO + tm, :] = _glu_in(h, gmix_ref[...], win_ref)

    shift_rows = tm + HALO - SUBLANES
    for b in range(1, SUBLANES):
        xs_scr[b - 1] = xbuf[b:b + shift_rows, :]

    def conv_slab(col, anchor):
        cs = slice(col * LANES, (col + 1) * LANES)
        taps = [wdw_ref[w:w + 1, cs] for w in range(CONV_WIDTH)]
        bias = jnp.broadcast_to(bdw_ref[:, cs], (CONV_ROWS, LANES))
        for r0 in range(0, tm, CONV_ROWS):
            acc = jnp.where(g >= 0, bias, anchor)
            for w in range(CONV_WIDTH):
                a, b = divmod(HALO - keep + w, SUBLANES)
                rows = slice(r0 + SUBLANES * a, r0 + SUBLANES * a + CONV_ROWS)
                x = xbuf[rows, cs] if b == 0 else xs_scr[b - 1, rows, cs]
                acc = acc + x * jnp.where(g >= 0, taps[w], anchor[0:1])
            y_scr[r0:r0 + CONV_ROWS, cs] = acc
            anchor = acc
        return anchor

    h1 = h1_scr[(g + 1) % 2]
    n = _rmsnorm(h1, gffn_ref[...]).astype(BF16)
    acc = h1
    slabs_per_chunk = n_slabs // n_chunks
    for c in range(n_chunks):
        a = jnp.maximum(_dot(n, wup_ref[:, c * FF_CHUNK:(c + 1) * FF_CHUNK]), 0.0)
        anchor = a[0:CONV_ROWS, 0:LANES]
        for i in range(slabs_per_chunk):
            anchor = conv_slab(c * slabs_per_chunk + i, anchor)
        acc = acc + _dot((a * a).astype(BF16), wdn_ref[c * FF_CHUNK:(c + 1) * FF_CHUNK, :])
    o_ref[...] = acc

    cc_ref[0] = xbuf[HALO + tm - keep:HALO + tm, :]
    xbuf[0:HALO, :] = xbuf[tm:tm + HALO, :]
    z = _layernorm(y_scr[...], cg_ref[...], cb_ref[...])
    z = (z * jax.nn.sigmoid(z)).astype(BF16)
    h1_scr[g % 2] = h + _dot(z, wout_ref[...])


def _conv_ffn_sample_kernel(h_ref, gmix_ref, win_ref, wdw_ref, bdw_ref, cg_ref, cb_ref, wout_ref, gffn_ref,
                            wup_ref, wdn_ref, st_ref, o_ref, cc_ref):
    keep = CONV_WIDTH - 1
    h = h_ref[...]
    x = _glu_in(h, gmix_ref[...], win_ref)
    y = x * wdw_ref[keep:keep + 1, :] + bdw_ref[...]
    for w in range(keep):
        y = y + st_ref[w] * wdw_ref[w:w + 1, :]
    for w in range(keep - 1):
        cc_ref[w] = st_ref[w + 1]
    cc_ref[keep - 1] = x
    o_ref[...] = _conv_tail(h, y, cg_ref, cb_ref, wout_ref, gffn_ref, wup_ref, wdn_ref)


def _conv_weight_specs(j):
    return [
        _const_spec((1, D_MODEL)), _layer_spec((D_MODEL, 2 * D_MODEL), j), _const_spec((CONV_WIDTH, D_MODEL)),
        _const_spec((1, D_MODEL)), _const_spec((1, D_MODEL)), _const_spec((1, D_MODEL)),
        _layer_spec((D_MODEL, D_MODEL), j), _const_spec((1, D_MODEL)),
        _const_spec((D_MODEL, D_FF)), _const_spec((D_FF, D_MODEL)),
    ]


def _conv_ffn_prompt(h, weights, j, batch, seq, tm, cast=None):
    keep = CONV_WIDTH - 1
    n_s = seq // tm
    n_tiles = batch * n_s
    conv_tile = lambda g: (jnp.minimum(g, n_tiles - 1), 0)
    ffn_tile = lambda g: (jnp.maximum(g - 1, 0), 0)
    body = functools.partial(_conv_ffn_prompt_kernel, tm=tm, tiles_per_seq=n_s)
    in_specs = [pl.BlockSpec((tm, D_MODEL), conv_tile)] + _conv_weight_specs(j)
    out_specs = [pl.BlockSpec((tm, D_MODEL), ffn_tile),
                 pl.BlockSpec((1, keep, D_MODEL), lambda g: (jnp.minimum(g, n_tiles - 1) // n_s, 0, 0))]
    out_shape = [jax.ShapeDtypeStruct((batch * seq, D_MODEL), F32),
                 jax.ShapeDtypeStruct((batch, keep, D_MODEL), F32)]
    operands = [h, *weights]
    if cast is not None:
        stacks, cast_layer = cast
        body = _with_cast_duty(body, len(in_specs), len(out_specs))
        c_in, c_out, c_shape = _cast_duty_specs(stacks, cast_layer, n_tiles, lambda g: jnp.minimum(g, n_tiles - 1))
        in_specs, out_specs, out_shape = in_specs + c_in, out_specs + c_out, out_shape + c_shape
        operands += list(stacks)
    return pl.pallas_call(
        body,
        grid=(n_tiles + 1,),
        in_specs=in_specs,
        out_specs=out_specs,
        out_shape=out_shape,
        scratch_shapes=[pltpu.VMEM((HALO + tm, D_MODEL), F32),
                        pltpu.VMEM((SUBLANES - 1, HALO + tm - SUBLANES, D_MODEL), F32),
                        pltpu.VMEM((tm, D_MODEL), F32),
                        pltpu.VMEM((2, tm, D_MODEL), F32)],
        compiler_params=pltpu.CompilerParams(dimension_semantics=("arbitrary",), vmem_limit_bytes=VMEM_LIMIT),
        name="conv_ffn",
    )(*operands)


def _conv_ffn_sample(h, weights, j, state_t):
    n = h.shape[0]
    keep = CONV_WIDTH - 1
    return pl.pallas_call(
        _conv_ffn_sample_kernel,
        grid=(1,),
        in_specs=[_const_spec((n, D_MODEL))] + _conv_weight_specs(j)
        + [pl.BlockSpec((None, keep, n, D_MODEL), lambda i: (j, 0, 0, 0))],
        out_specs=[_const_spec((n, D_MODEL)), _const_spec((keep, n, D_MODEL))],
        out_shape=[jax.ShapeDtypeStruct((n, D_MODEL), F32), jax.ShapeDtypeStruct((keep, n, D_MODEL), F32)],
        compiler_params=pltpu.CompilerParams(dimension_semantics=("arbitrary",), vmem_limit_bytes=VMEM_LIMIT),
        name="conv_ffn_sample",
    )(h, *weights, state_t)


def _rope_tables(pos):
    inv = ROPE_THETA ** (-jnp.arange(0, HEAD_DIM, 2, dtype=F32) / HEAD_DIM)
    ang = pos.astype(F32)[:, None] * inv[None, :]
    cos, sin = jnp.cos(ang), jnp.sin(ang)
    reps = LANES // HEAD_DIM
    return jnp.tile(jnp.concatenate([cos, cos], axis=-1), (1, reps)), jnp.tile(jnp.concatenate([-sin, sin], axis=-1), (1, reps))


def kernel(x_prompt, x_sample, cache_a_k, cache_a_v, state_c_conv, norm_mix_g, norm_ffn_g, w_ffn_up, w_ffn_down, w_in_ab, q_norm_g, k_norm_g, vb_norm_g, vb_norm_b, w_spatial, b_spatial, w_out_ab, w_c_in, w_c_dw, b_c_dw, c_norm_g, c_norm_b, w_c_out):
    batch, seq, _ = x_prompt.shape
    n_dec, dec_seq, _ = x_sample.shape
    depth = norm_mix_g.shape[0]
    n_ab, _, win_buf = cache_a_k.shape[:3]
    past_len = PAST_LEN
    assert dec_seq == 1 and seq % (BRANCHES[-1][1] * Q_BLOCK) == 0 and seq % CHUNK == 0
    tm = 256

    hp = x_prompt.reshape(batch * seq, D_MODEL)
    hs = x_sample.reshape(n_dec * dec_seq, D_MODEL)
    cos_p, sin_p = _rope_tables(jnp.arange(seq, dtype=jnp.int32))
    cos_s, sin_s = _rope_tables(jnp.full((n_dec,), past_len, dtype=jnp.int32))
    head_id = jnp.arange(LANES) // HEAD_DIM
    head_mean = jnp.where(head_id[:, None] == head_id[None, :], 1.0 / HEAD_DIM, 0.0).astype(BF16)
    cache_kt = cache_a_k.transpose(0, 1, 3, 4, 2)
    cache_vt = cache_a_v.transpose(0, 1, 3, 4, 2)
    state_t = state_c_conv.transpose(0, 2, 1, 3)
    group_dim = B_WIDTH // N_GROUPS
    row2 = lambda a: a.reshape(1, -1)

    w_in, wout = w_in_ab.astype(BF16), w_out_ab.astype(BF16)
    wc_in, wc_out = w_c_in.astype(BF16), w_c_out.astype(BF16)
    ffn_f32 = (w_ffn_up, w_ffn_down)
    ffn_bf = {}
    next_cast = lambda layer: (ffn_f32, layer + 1) if layer + 1 < depth else None

    ak_s, av_s, bv_p, bv_s, cc_p, cc_s = [], [], [], [], [], []
    kv_t = ()
    for layer in range(depth):
        j = layer // 2
        gffn = row2(norm_ffn_g[layer])
        gmix = row2(norm_mix_g[layer])
        if layer % 2 == 0:
            qg = row2(jnp.tile(q_norm_g[j], N_HEADS))
            kg = row2(jnp.tile(k_norm_g[j], N_HEADS))
            vbg, vbb = row2(vb_norm_g[j]), row2(vb_norm_b[j])
            qp, kp, vp, up, vbp, *rest = _ab_in(hp, gmix, w_in, j, cos_p, sin_p, qg, kg, head_mean, vbg, vbb, tm,
                                                seq=seq, prev_t=tuple(kv_t),
                                                cast=None if layer in ffn_bf else (ffn_f32, layer))
            kv_t = rest[:2]
            if layer not in ffn_bf:
                ffn_bf[layer] = tuple(rest[2:])
            wup, wdn = ffn_bf[layer]
            qs, ks, vs, us, vbs = _ab_in(hs, gmix, w_in, j, cos_s, sin_s, qg, kg, head_mean, vbg, vbb, n_dec)
            shp = (batch, seq, A_WIDTH)
            att_p, att_s = _attn(qp.reshape(shp), kp.reshape(shp), vp.reshape(shp), qs, ks, vs, cache_kt, cache_vt, j)
            att_p = att_p.reshape(batch * seq, A_WIDTH)
            w_pairs = w_spatial[j].reshape(N_GROUPS // 2, 2, CHUNK, CHUNK).transpose(0, 2, 1, 3).reshape(N_GROUPS // 2, CHUNK, 2 * CHUNK)
            b_rows = jnp.repeat(b_spatial[j].T, group_dim, axis=1)
            w0 = row2(jnp.repeat(w_spatial[j][:, 0, 0], group_dim))
            b0 = row2(jnp.repeat(b_spatial[j][:, 0], group_dim))
            hp, *cast_out = _mix_ffn(hp, att_p, up, vbp, w_pairs, b_rows, wout, j, gffn, wup, wdn, tm, False,
                                     cast=next_cast(layer))
            hs, = _mix_ffn(hs, att_s, us, vbs, w0, b0, wout, j, gffn, wup, wdn, n_dec, True)
            ak_s.append(ks.reshape(n_dec, dec_seq, N_HEADS, HEAD_DIM))
            av_s.append(vs.reshape(n_dec, dec_seq, N_HEADS, HEAD_DIM))
            last_chunk_start = ((seq - 1) // CHUNK) * CHUNK
            bv_p.append(vbp.reshape(batch, seq, B_WIDTH)[:, last_chunk_start:])
            bv_s.append(vbs.reshape(n_dec, dec_seq, B_WIDTH))
        else:
            wup, wdn = ffn_bf[layer]
            weights = (gmix, wc_in, w_c_dw[j], row2(b_c_dw[j]), row2(c_norm_g[j]),
                       row2(c_norm_b[j]), wc_out, gffn, wup, wdn)
            hp, new_cp, *cast_out = _conv_ffn_prompt(hp, weights, j, batch, seq, tm, cast=next_cast(layer))
            hs, new_cs = _conv_ffn_sample(hs, weights, j, state_t)
            cc_p.append(new_cp)
            cc_s.append(new_cs)
        if cast_out:
            ffn_bf[layer + 1] = tuple(cast_out)

    prompt_buf = min(BRANCHES[-1][0], seq)
    ak_p, av_p = [a.reshape(n_ab, batch, N_HEADS, HEAD_DIM, seq).transpose(0, 1, 4, 2, 3)[:, :, seq - prompt_buf:]
                  for a in kv_t]
    return (hp.reshape(batch, seq, D_MODEL), hs.reshape(n_dec, dec_seq, D_MODEL),
            ak_p, av_p, jnp.stack(ak_s), jnp.stack(av_s),
            jnp.stack(bv_p), jnp.stack(bv_s), jnp.stack(cc_p), jnp.stack(cc_s).transpose(0, 2, 1, 3))
```

```python
import functools

import jax
import jax.numpy as jnp
from jax import lax
from jax.experimental import pallas as pl
from jax.experimental.pallas import tpu as pltpu

F32 = jnp.float32
BF16 = jnp.bfloat16

D_MODEL = 1024
N_HEADS = 8
HEAD_DIM = 64
A_WIDTH = N_HEADS * HEAD_DIM
B_WIDTH = 512
N_GROUPS = 8
CHUNK = 128
Q_BLOCK = 128
BRANCHES = ((128, 1), (512, 4), (2048, 16))
ROPE_THETA = 10000.0
CONV_WIDTH = 31
PAST_LEN = 8192
D_FF = 4 * D_MODEL
EPS = 1e-6
IN_AB = 3 * A_WIDTH + 2 * B_WIDTH

LANES = 128
SUBLANES = 8
HALO = 32
CONV_ROWS = 128
FF_CHUNK = 1024
NEG = -1e30
LOG2_E = 1.4426950408889634
QK_SCALE = HEAD_DIM ** -0.5 * LOG2_E
ATTN_GROUP = 16
VMEM_LIMIT = 56 * 1024 * 1024


def _rmsnorm(x, g):
    ms = jnp.mean(x * x, axis=-1, keepdims=True)
    return x * lax.rsqrt(ms + EPS) * g


def _layernorm(x, g, b):
    mu = jnp.mean(x, axis=-1, keepdims=True)
    xc = x - mu
    var = jnp.mean(xc * xc, axis=-1, keepdims=True)
    return xc * lax.rsqrt(var + EPS) * g + b


def _dot(a, b):
    return jnp.dot(a, b, preferred_element_type=F32)


def _ffn(h1, g, wup_ref, wdn_ref):
    n = _rmsnorm(h1, g).astype(BF16)
    acc = h1
    for c in range(D_FF // FF_CHUNK):
        a = _dot(n, wup_ref[:, c * FF_CHUNK:(c + 1) * FF_CHUNK])
        a = jnp.maximum(a, 0.0)
        a = (a * a).astype(BF16)
        acc = acc + _dot(a, wdn_ref[c * FF_CHUNK:(c + 1) * FF_CHUNK, :])
    return acc


def _ab_in_kernel(h_ref, g_ref, w_ref, cos_ref, sin_ref, qg_ref, kg_ref, hm_ref, vbg_ref, vbb_ref, *refs,
                  n_prev, transposed):
    prev_refs, refs = refs[:2 if n_prev else 0], refs[2 if n_prev else 0:]
    q_ref, k_ref, v_ref, u_ref, vb_ref = refs[:5]
    kt_ref, vt_ref = refs[5:] if transposed else (None, None)
    if n_prev:
        kt_ref[0:n_prev] = prev_refs[0][...]
        vt_ref[0:n_prev] = prev_refs[1][...]
    n = _rmsnorm(h_ref[...], g_ref[...]).astype(BF16)
    cos = cos_ref[...]
    sin = sin_ref[...]
    hm = hm_ref[...]
    lane = lax.broadcasted_iota(jnp.int32, (1, LANES), 1)
    first_half = (lane & (HEAD_DIM // 2)) == 0

    def head_norm_rope(z, gain_ref, out_ref, scale, t_ref):
        for c in range(A_WIDTH // LANES):
            sl = slice(c * LANES, (c + 1) * LANES)
            zc = z[:, sl]
            sq = zc * zc
            hi = sq.astype(BF16)
            lo = (sq - hi.astype(F32)).astype(BF16)
            ms = _dot(hi, hm) + _dot(lo, hm)
            y = zc * lax.rsqrt(ms + EPS) * gain_ref[:, sl]
            partner = jnp.where(first_half, pltpu.roll(y, LANES - HEAD_DIM // 2, 1),
                                pltpu.roll(y, HEAD_DIM // 2, 1))
            out = y * cos + partner * sin
            if scale != 1.0:
                out = out * scale
            out_ref[:, sl] = out
            if t_ref is not None:
                t_ref[n_prev, sl, :] = out.T

    head_norm_rope(_dot(n, w_ref[:, 0:A_WIDTH]), qg_ref, q_ref, QK_SCALE, None)
    head_norm_rope(_dot(n, w_ref[:, A_WIDTH:2 * A_WIDTH]), kg_ref, k_ref, 1.0, kt_ref)
    v = _dot(n, w_ref[:, 2 * A_WIDTH:3 * A_WIDTH])
    v_ref[...] = v
    if transposed:
        for c in range(A_WIDTH // LANES):
            sl = slice(c * LANES, (c + 1) * LANES)
            vt_ref[n_prev, sl, :] = v[:, sl].T
    u_ref[...] = jax.nn.gelu(_dot(n, w_ref[:, 3 * A_WIDTH:3 * A_WIDTH + B_WIDTH]))
    vb = jax.nn.gelu(_dot(n, w_ref[:, 3 * A_WIDTH + B_WIDTH:IN_AB]))
    vb_ref[...] = _layernorm(vb, vbg_ref[...], vbb_ref[...])


def _const_spec(shape):
    nd = len(shape)
    return pl.BlockSpec(shape, lambda *_: (0,) * nd, pipeline_mode=pl.Buffered(1))


def _layer_spec(shape, layer):
    nd = len(shape)
    return pl.BlockSpec((None, *shape), lambda *_: (layer,) + (0,) * nd, pipeline_mode=pl.Buffered(1))


def _with_cast_duty(body, n_in, n_out):
    def kernel(*refs):
        ins, srcs = refs[:n_in], refs[n_in:n_in + 2]
        outs, dsts = refs[n_in + 2:n_in + 2 + n_out], refs[n_in + 2 + n_out:n_in + 4 + n_out]
        for src, dst in zip(srcs, dsts):
            dst[...] = src[...].astype(BF16)
        body(*ins, *outs, *refs[n_in + 4 + n_out:])
    return kernel


def _cast_duty_specs(stacks, layer, n_steps, step_of):
    in_specs, out_specs, out_shape = [], [], []
    for w in stacks:
        _, rows, cols = w.shape
        per_step = rows // n_steps
        assert per_step * n_steps == rows and per_step % 16 == 0
        in_specs.append(pl.BlockSpec((None, per_step, cols), lambda *ids: (layer, step_of(*ids), 0)))
        out_specs.append(pl.BlockSpec((per_step, cols), lambda *ids: (step_of(*ids), 0)))
        out_shape.append(jax.ShapeDtypeStruct((rows, cols), BF16))
    return in_specs, out_specs, out_shape


def _ab_in(h, g, w_bf, j, cos_t, sin_t, qg, kg, hm, vbg, vbb, tm, seq=None, prev_t=(), cast=None):
    t = h.shape[0]
    n_pos_blocks = cos_t.shape[0] // tm
    row = lambda i: (i, 0)
    tab = lambda i: (i % n_pos_blocks, 0)
    out = jax.ShapeDtypeStruct((t, A_WIDTH), F32)
    in_specs = [
        pl.BlockSpec((tm, D_MODEL), row), _const_spec((1, D_MODEL)), _layer_spec((D_MODEL, IN_AB), j),
        pl.BlockSpec((tm, LANES), tab), pl.BlockSpec((tm, LANES), tab),
        _const_spec((1, A_WIDTH)), _const_spec((1, A_WIDTH)), _const_spec((LANES, LANES)),
        _const_spec((1, B_WIDTH)), _const_spec((1, B_WIDTH)),
    ]
    out_specs = [pl.BlockSpec((tm, A_WIDTH), row)] * 5
    out_shape = [out] * 5
    n_prev = prev_t[0].shape[0] if prev_t else 0
    if seq is not None:
        tiles_per_seq = seq // tm
        t_map = lambda i: (0, i // tiles_per_seq, 0, i % tiles_per_seq)
        if prev_t:
            in_specs += [pl.BlockSpec((n_prev, None, A_WIDTH, tm), t_map)] * 2
        out_specs += [pl.BlockSpec((n_prev + 1, None, A_WIDTH, tm), t_map)] * 2
        out_shape += [jax.ShapeDtypeStruct((n_prev + 1, t // seq, A_WIDTH, seq), F32)] * 2
    body = functools.partial(_ab_in_kernel, n_prev=n_prev, transposed=seq is not None)
    operands = [h, g, w_bf, cos_t, sin_t, qg, kg, hm, vbg, vbb, *prev_t]
    if cast is not None:
        stacks, cast_layer = cast
        body = _with_cast_duty(body, len(in_specs), len(out_specs))
        c_in, c_out, c_shape = _cast_duty_specs(stacks, cast_layer, t // tm, lambda i: i)
        in_specs, out_specs, out_shape = in_specs + c_in, out_specs + c_out, out_shape + c_shape
        operands += list(stacks)
    return pl.pallas_call(
        body,
        grid=(t // tm,),
        in_specs=in_specs,
        out_specs=out_specs,
        out_shape=out_shape,
        compiler_params=pltpu.CompilerParams(dimension_semantics=("arbitrary",), vmem_limit_bytes=VMEM_LIMIT),
        name="ab_in",
    )(*operands)


def _attn_prompt_kernel(q_ref, k_ref, v_ref, o_ref, ob_scr, lse_scr, *, seq):
    lane = lax.broadcasted_iota(jnp.int32, (1, LANES), 1)
    head0 = lane < HEAD_DIM
    ri = lax.broadcasted_iota(jnp.int32, (2 * Q_BLOCK, 2 * Q_BLOCK), 0) & (Q_BLOCK - 1)
    ci = lax.broadcasted_iota(jnp.int32, (2 * Q_BLOCK, 2 * Q_BLOCK), 1)
    band_two = jnp.where(ci < Q_BLOCK, ci - ri, ri + Q_BLOCK - ci) >= 0
    band_one = (lax.broadcasted_iota(jnp.int32, (2 * Q_BLOCK, Q_BLOCK), 1)
                <= lax.broadcasted_iota(jnp.int32, (2 * Q_BLOCK, Q_BLOCK), 0) & (Q_BLOCK - 1))

    def rows(start, dil):
        if dil == 1:
            return pl.ds(start if isinstance(start, int) else pl.multiple_of(start, Q_BLOCK), Q_BLOCK)
        return pl.ds(start, Q_BLOCK, stride=dil)

    def load(dil, start, with_prev):
        cur = rows(start, dil)
        q = q_ref[0, cur, :]
        q2 = jnp.concatenate([jnp.where(head0, q, 0.0), jnp.where(head0, 0.0, q)], axis=0).astype(BF16)
        k = k_ref[0, cur, :].astype(BF16)
        v = v_ref[0, cur, :].astype(BF16)
        if with_prev:
            prev = rows(start - dil * Q_BLOCK, dil)
            k = jnp.concatenate([k_ref[0, prev, :].astype(BF16), k], axis=0)
            v = jnp.concatenate([v_ref[0, prev, :].astype(BF16), v], axis=0)
        return q2, k, v

    def attend(q2, k, v, with_prev):
        n_keys = k.shape[0]
        s = lax.dot_general(q2, k, (((1,), (1,)), ((), ())), preferred_element_type=F32)
        s = jnp.where(band_two if with_prev else band_one, s, NEG)
        m = jnp.max(s, axis=-1, keepdims=True)
        p = jnp.exp2(s - m).astype(BF16)
        r = _dot(p, jnp.concatenate([v, jnp.ones((n_keys, LANES), BF16)], axis=1))
        den = r[:, LANES:]
        o = r[:, :LANES] * (1.0 / den)
        lse = m + jnp.log2(den)
        return jnp.where(head0, o[:Q_BLOCK], o[Q_BLOCK:]), jnp.where(head0, lse[:Q_BLOCK], lse[Q_BLOCK:])

    def blocks(branch, dil, starts, with_prev):
        loaded = [load(dil, st, with_prev) for st in starts]
        results = [attend(*ld, with_prev) for ld in loaded]
        for st, (o, lse) in zip(starts, results):
            ob_scr[branch, rows(st, dil), :] = o
            lse_scr[branch, rows(st, dil), :] = lse

    def grouped_loop(total, body):
        n_full = total // ATTN_GROUP

        def loop_body(it, carry):
            body([it * ATTN_GROUP + g for g in range(ATTN_GROUP)])
            return carry

        if n_full:
            lax.fori_loop(0, n_full, loop_body, 0)
        if total % ATTN_GROUP:
            body(list(range(n_full * ATTN_GROUP, total)))

    for branch, (window, dil) in enumerate(BRANCHES):
        assert window // dil == Q_BLOCK
        sub_len = seq // dil
        n_blocks = sub_len // Q_BLOCK

        def first_body(items, branch=branch, dil=dil):
            blocks(branch, dil, items, False)

        grouped_loop(dil, first_body)
        if n_blocks > 1:
            def rest_body(items, branch=branch, dil=dil, n_blocks=n_blocks):
                starts = [it // (n_blocks - 1) + dil * Q_BLOCK * (it % (n_blocks - 1) + 1) for it in items]
                blocks(branch, dil, starts, True)

            grouped_loop(dil * (n_blocks - 1), rest_body)

    rows_per_step = 256

    def merge_body(it, carry):
        sl = pl.ds(pl.multiple_of(it * rows_per_step, rows_per_step), rows_per_step)
        lses = [lse_scr[b, sl, :] for b in range(len(BRANCHES))]
        m = functools.reduce(jnp.maximum, lses)
        ws = [jnp.exp2(l - m) for l in lses]
        num = sum(w * ob_scr[b, sl, :] for b, w in enumerate(ws))
        o_ref[0, sl, :] = (num * (1.0 / sum(ws))).astype(o_ref.dtype)
        return carry

    lax.fori_loop(0, seq // rows_per_step, merge_body, 0)


def _attn(q, k, v, q_s, k_s, v_s, cache_kt, cache_vt, layer):
    b, seq, _ = q.shape
    n_slabs = A_WIDTH // LANES
    n, win_buf = cache_kt.shape[1], cache_kt.shape[-1]
    assert n == b * n_slabs
    for window, dil in BRANCHES:
        assert window <= win_buf <= PAST_LEN and dil & (dil - 1) == 0
    spec = pl.BlockSpec((1, seq, LANES), lambda i, j: (i, 0, j))
    rows = pl.BlockSpec((n, A_WIDTH), lambda i, j: (0, 0))
    cache = pl.BlockSpec((None, None, N_HEADS, HEAD_DIM, win_buf), lambda i, j: (layer, i * n_slabs + j, 0, 0, 0))
    prompt_scratch = [pltpu.VMEM((len(BRANCHES), seq, LANES), F32), pltpu.VMEM((len(BRANCHES), seq, LANES), F32)]
    sample_scratch = [pltpu.VMEM((N_HEADS, win_buf), F32), pltpu.VMEM((N_HEADS, LANES), F32),
                      pltpu.VMEM((N_HEADS, win_buf), F32)]

    def kernel(*refs):
        prompt_in, sample_in, (o_ref, os_ref), scratch = refs[:3], refs[3:8], refs[8:10], refs[10:]
        step = pl.program_id(0) * n_slabs + pl.program_id(1)
        _attn_sample_kernel(step, *sample_in, os_ref, *scratch[len(prompt_scratch):], win_buf=win_buf)
        _attn_prompt_kernel(*prompt_in, o_ref, *scratch[:len(prompt_scratch)], seq=seq)

    return pl.pallas_call(
        kernel,
        grid=(b, n_slabs),
        in_specs=[spec, spec, spec, rows, rows, rows, cache, cache],
        out_specs=[spec, rows],
        out_shape=[jax.ShapeDtypeStruct((b, seq, A_WIDTH), BF16), jax.ShapeDtypeStruct((n, A_WIDTH), F32)],
        scratch_shapes=prompt_scratch + sample_scratch,
        compiler_params=pltpu.CompilerParams(dimension_semantics=("arbitrary", "arbitrary"),
                                             vmem_limit_bytes=VMEM_LIMIT),
        name="attn",
    )(q, k, v, q_s, k_s, v_s, cache_kt, cache_vt)


def _attn_sample_kernel(n, q_ref, kn_ref, vn_ref, kt_ref, vt_ref, o_ref, s_scr, sn_scr, p_scr, *, win_buf):
    heads_per_slab = LANES // HEAD_DIM

    def head_columns(ref):
        r = ref[pl.ds(n, 1), :]
        slabs = [jnp.broadcast_to(r[:, c * LANES:(c + 1) * LANES], (LANES, LANES)).T for c in range(A_WIDTH // LANES)]
        return [slabs[h // heads_per_slab][(h % heads_per_slab) * HEAD_DIM:(h % heads_per_slab + 1) * HEAD_DIM, 0:1]
                for h in range(N_HEADS)]

    q_cols, kn_cols, vn_cols = head_columns(q_ref), head_columns(kn_ref), head_columns(vn_ref)
    back = win_buf - lax.broadcasted_iota(jnp.int32, (1, win_buf), 1)
    masks = [jnp.where((back & (dil - 1)) == 0, back, window + 1) <= window for window, dil in BRANCHES]
    for h in range(N_HEADS):
        q = q_cols[h]
        s_scr[h:h + 1, :] = jnp.sum(kt_ref[h] * q, axis=0, keepdims=True)
        sn_scr[h:h + 1, :] = jnp.broadcast_to(jnp.sum(q * kn_cols[h], axis=0, keepdims=True), (1, LANES))
    s = s_scr[...]
    s_new = sn_scr[:, 0:1]
    ps, pns, lses = [], [], []
    for mask in masks:
        sb = jnp.where(mask, s, NEG)
        m = jnp.maximum(jnp.max(sb, axis=1, keepdims=True), s_new)
        p = jnp.exp2(sb - m)
        pn = jnp.exp2(s_new - m)
        den = jnp.sum(p, axis=1, keepdims=True) + pn
        inv = 1.0 / den
        ps.append(p * inv)
        pns.append(pn * inv)
        lses.append(m + jnp.log2(den))
    mm = functools.reduce(jnp.maximum, lses)
    ws = [jnp.exp2(l - mm) for l in lses]
    inv = 1.0 / sum(ws)
    p_scr[...] = sum(w * p for w, p in zip(ws, ps)) * inv
    pn_all = sum(w * pn for w, pn in zip(ws, pns)) * inv
    o_cols = [jnp.sum(vt_ref[h] * p_scr[h:h + 1, :], axis=1, keepdims=True) + pn_all[h:h + 1] * vn_cols[h]
              for h in range(N_HEADS)]
    out = []
    for c in range(A_WIDTH // LANES):
        col = jnp.concatenate(o_cols[c * heads_per_slab:(c + 1) * heads_per_slab], axis=0)
        out.append(jnp.broadcast_to(col, (LANES, LANES)).T[0:1, :])
    o_ref[pl.ds(n, 1), :] = jnp.concatenate(out, axis=1)


def _mix_ffn_prompt_kernel(h_ref, att_ref, u_ref, vb_ref, wsp_ref, bsp_ref, wout_ref, gffn_ref, wup_ref, wdn_ref,
                           o_ref, gate_scr, *, tm):
    lane = lax.broadcasted_iota(jnp.int32, (1, LANES), 1)
    lo = lane < B_WIDTH // N_GROUPS
    ri = lax.broadcasted_iota(jnp.int32, (CHUNK, 2 * CHUNK), 0)
    ci = lax.broadcasted_iota(jnp.int32, (CHUNK, 2 * CHUNK), 1)
    causal = (ci & (CHUNK - 1)) <= ri
    n_pairs = B_WIDTH // LANES
    w_pairs = [jnp.where(causal, wsp_ref[p], 0.0).astype(BF16) for p in range(n_pairs)]
    for c in range(tm // CHUNK):
        rows = slice(c * CHUNK, (c + 1) * CHUNK)
        for p in range(n_pairs):
            sl = slice(p * LANES, (p + 1) * LANES)
            slab = vb_ref[rows, sl]
            rhs = jnp.concatenate([jnp.where(lo, slab, 0.0), jnp.where(lo, 0.0, slab)], axis=0).astype(BF16)
            mixed = _dot(w_pairs[p], rhs) + bsp_ref[:, sl]
            gate_scr[rows, sl] = (u_ref[rows, sl] * mixed).astype(BF16)
    mix = _dot(att_ref[...], wout_ref[0:A_WIDTH, :]) + _dot(gate_scr[...], wout_ref[A_WIDTH:, :])
    h1 = h_ref[...] + mix
    o_ref[...] = _ffn(h1, gffn_ref[...], wup_ref, wdn_ref)


def _mix_ffn_sample_kernel(h_ref, att_ref, u_ref, vb_ref, w0_ref, b0_ref, wout_ref, gffn_ref, wup_ref, wdn_ref,
                           o_ref):
    gate = (u_ref[...] * (vb_ref[...] * w0_ref[...] + b0_ref[...])).astype(BF16)
    mix = _dot(att_ref[...].astype(BF16), wout_ref[0:A_WIDTH, :]) + _dot(gate, wout_ref[A_WIDTH:, :])
    h1 = h_ref[...] + mix
    o_ref[...] = _ffn(h1, gffn_ref[...], wup_ref, wdn_ref)


def _mix_ffn(h, att, u, vb, gate_w, gate_b, wout_bf, j, gffn, wup_bf, wdn_bf, tm, sample, cast=None):
    t = h.shape[0]
    row = lambda i: (i, 0)
    if sample:
        body = _mix_ffn_sample_kernel
        scratch = []
    else:
        body = functools.partial(_mix_ffn_prompt_kernel, tm=tm)
        scratch = [pltpu.VMEM((tm, B_WIDTH), BF16)]
    in_specs = [
        pl.BlockSpec((tm, D_MODEL), row), pl.BlockSpec((tm, A_WIDTH), row),
        pl.BlockSpec((tm, B_WIDTH), row), pl.BlockSpec((tm, B_WIDTH), row),
        _const_spec(gate_w.shape), _const_spec(gate_b.shape),
        _layer_spec((A_WIDTH + B_WIDTH, D_MODEL), j), _const_spec((1, D_MODEL)),
        _const_spec((D_MODEL, D_FF)), _const_spec((D_FF, D_MODEL)),
    ]
    out_specs = [pl.BlockSpec((tm, D_MODEL), row)]
    out_shape = [jax.ShapeDtypeStruct((t, D_MODEL), F32)]
    operands = [h, att, u, vb, gate_w, gate_b, wout_bf, gffn, wup_bf, wdn_bf]
    if cast is not None:
        stacks, cast_layer = cast
        body = _with_cast_duty(body, len(in_specs), len(out_specs))
        c_in, c_out, c_shape = _cast_duty_specs(stacks, cast_layer, t // tm, lambda i: i)
        in_specs, out_specs, out_shape = in_specs + c_in, out_specs + c_out, out_shape + c_shape
        operands += list(stacks)
    return pl.pallas_call(
        body,
        grid=(t // tm,),
        in_specs=in_specs,
        out_specs=out_specs,
        out_shape=out_shape,
        scratch_shapes=scratch,
        compiler_params=pltpu.CompilerParams(dimension_semantics=("arbitrary",), vmem_limit_bytes=VMEM_LIMIT),
        name="mix_ffn_sample" if sample else "mix_ffn",
    )(*operands)


def _glu_in(h, gmix, win_ref):
    n = _rmsnorm(h, gmix).astype(BF16)
    a = _dot(n, win_ref[:, 0:D_MODEL])
    gate = _dot(n, win_ref[:, D_MODEL:])
    return a * jax.nn.sigmoid(gate)


def _conv_tail(h, y, cg_ref, cb_ref, wout_ref, gffn_ref, wup_ref, wdn_ref):
    z = _layernorm(y, cg_ref[...], cb_ref[...])
    z = (z * jax.nn.sigmoid(z)).astype(BF16)
    h1 = h + _dot(z, wout_ref[...])
    return _ffn(h1, gffn_ref[...], wup_ref, wdn_ref)


def _conv_ffn_prompt_kernel(h_ref, gmix_ref, win_ref, wdw_ref, bdw_ref, cg_ref, cb_ref, wout_ref, gffn_ref,
                            wup_ref, wdn_ref, o_ref, cc_ref, xbuf, xs_scr, y_scr, h1_scr, *, tm, tiles_per_seq):
    g = pl.program_id(0)
    keep = CONV_WIDTH - 1
    n_slabs = D_MODEL // LANES
    n_chunks = D_FF // FF_CHUNK

    @pl.when(g == 0)
    def _():
        h1_scr[...] = jnp.zeros(h1_scr.shape, F32)
        xbuf[0:HALO, :] = jnp.zeros((HALO, D_MODEL), F32)

    s = jnp.minimum(g, pl.num_programs(0) - 2) % tiles_per_seq
    xbuf[0:HALO, :] = jnp.where(s == 0, 0.0, xbuf[0:HALO, :])
    h = h_ref[...]
    xbuf[HALO:HALO + tm, :] = _glu_in(h, gmix_ref[...], win_ref)

    shift_rows = tm + HALO - SUBLANES
    for b in range(1, SUBLANES):
        xs_scr[b - 1] = xbuf[b:b + shift_rows, :]

    def conv_slab(col, anchor):
        cs = slice(col * LANES, (col + 1) * LANES)
        taps = [wdw_ref[w:w + 1, cs] for w in range(CONV_WIDTH)]
        bias = jnp.broadcast_to(bdw_ref[:, cs], (CONV_ROWS, LANES))
        for r0 in range(0, tm, CONV_ROWS):
            acc = jnp.where(g >= 0, bias, anchor)
            for w in range(CONV_WIDTH):
                a, b = divmod(HALO - keep + w, SUBLANES)
                rows = slice(r0 + SUBLANES * a, r0 + SUBLANES * a + CONV_ROWS)
                x = xbuf[rows, cs] if b == 0 else xs_scr[b - 1, rows, cs]
                acc = acc + x * jnp.where(g >= 0, taps[w], anchor[0:1])
            y_scr[r0:r0 + CONV_ROWS, cs] = acc
            anchor = acc
        return anchor

    h1 = h1_scr[(g + 1) % 2]
    n = _rmsnorm(h1, gffn_ref[...]).astype(BF16)
    acc = h1
    slabs_per_chunk = n_slabs // n_chunks
    for c in range(n_chunks):
        a = jnp.maximum(_dot(n, wup_ref[:, c * FF_CHUNK:(c + 1) * FF_CHUNK]), 0.0)
        anchor = a[0:CONV_ROWS, 0:LANES]
        for i in range(slabs_per_chunk):
            anchor = conv_slab(c * slabs_per_chunk + i, anchor)
        acc = acc + _dot((a * a).astype(BF16), wdn_ref[c * FF_CHUNK:(c + 1) * FF_CHUNK, :])
    o_ref[...] = acc

    cc_ref[0] = xbuf[HALO + tm - keep:HALO + tm, :]
    xbuf[0:HALO, :] = xbuf[tm:tm + HALO, :]
    z = _layernorm(y_scr[...], cg_ref[...], cb_ref[...])
    z = (z * jax.nn.sigmoid(z)).astype(BF16)
    h1_scr[g % 2] = h + _dot(z, wout_ref[...])


def _conv_ffn_sample_kernel(h_ref, gmix_ref, win_ref, wdw_ref, bdw_ref, cg_ref, cb_ref, wout_ref, gffn_ref,
                            wup_ref, wdn_ref, st_ref, o_ref, cc_ref):
    keep = CONV_WIDTH - 1
    h = h_ref[...]
    x = _glu_in(h, gmix_ref[...], win_ref)
    y = x * wdw_ref[keep:keep + 1, :] + bdw_ref[...]
    for w in range(keep):
        y = y + st_ref[w] * wdw_ref[w:w + 1, :]
    for w in range(keep - 1):
        cc_ref[w] = st_ref[w + 1]
    cc_ref[keep - 1] = x
    o_ref[...] = _conv_tail(h, y, cg_ref, cb_ref, wout_ref, gffn_ref, wup_ref, wdn_ref)


def _conv_weight_specs(j):
    return [
        _const_spec((1, D_MODEL)), _layer_spec((D_MODEL, 2 * D_MODEL), j), _const_spec((CONV_WIDTH, D_MODEL)),
        _const_spec((1, D_MODEL)), _const_spec((1, D_MODEL)), _const_spec((1, D_MODEL)),
        _layer_spec((D_MODEL, D_MODEL), j), _const_spec((1, D_MODEL)),
        _const_spec((D_MODEL, D_FF)), _const_spec((D_FF, D_MODEL)),
    ]


def _conv_ffn_prompt(h, weights, j, batch, seq, tm, cast=None):
    keep = CONV_WIDTH - 1
    n_s = seq // tm
    n_tiles = batch * n_s
    conv_tile = lambda g: (jnp.minimum(g, n_tiles - 1), 0)
    ffn_tile = lambda g: (jnp.maximum(g - 1, 0), 0)
    body = functools.partial(_conv_ffn_prompt_kernel, tm=tm, tiles_per_seq=n_s)
    in_specs = [pl.BlockSpec((tm, D_MODEL), conv_tile)] + _conv_weight_specs(j)
    out_specs = [pl.BlockSpec((tm, D_MODEL), ffn_tile),
                 pl.BlockSpec((1, keep, D_MODEL), lambda g: (jnp.minimum(g, n_tiles - 1) // n_s, 0, 0))]
    out_shape = [jax.ShapeDtypeStruct((batch * seq, D_MODEL), F32),
                 jax.ShapeDtypeStruct((batch, keep, D_MODEL), F32)]
    operands = [h, *weights]
    if cast is not None:
        stacks, cast_layer = cast
        body = _with_cast_duty(body, len(in_specs), len(out_specs))
        c_in, c_out, c_shape = _cast_duty_specs(stacks, cast_layer, n_tiles, lambda g: jnp.minimum(g, n_tiles - 1))
        in_specs, out_specs, out_shape = in_specs + c_in, out_specs + c_out, out_shape + c_shape
        operands += list(stacks)
    return pl.pallas_call(
        body,
        grid=(n_tiles + 1,),
        in_specs=in_specs,
        out_specs=out_specs,
        out_shape=out_shape,
        scratch_shapes=[pltpu.VMEM((HALO + tm, D_MODEL), F32),
                        pltpu.VMEM((SUBLANES - 1, HALO + tm - SUBLANES, D_MODEL), F32),
                        pltpu.VMEM((tm, D_MODEL), F32),
                        pltpu.VMEM((2, tm, D_MODEL), F32)],
        compiler_params=pltpu.CompilerParams(dimension_semantics=("arbitrary",), vmem_limit_bytes=VMEM_LIMIT),
        name="conv_ffn",
    )(*operands)


def _conv_ffn_sample(h, weights, j, state_t):
    n = h.shape[0]
    keep = CONV_WIDTH - 1
    return pl.pallas_call(
        _conv_ffn_sample_kernel,
        grid=(1,),
        in_specs=[_const_spec((n, D_MODEL))] + _conv_weight_specs(j)
        + [pl.BlockSpec((None, keep, n, D_MODEL), lambda i: (j, 0, 0, 0))],
        out_specs=[_const_spec((n, D_MODEL)), _const_spec((keep, n, D_MODEL))],
        out_shape=[jax.ShapeDtypeStruct((n, D_MODEL), F32), jax.ShapeDtypeStruct((keep, n, D_MODEL), F32)],
        compiler_params=pltpu.CompilerParams(dimension_semantics=("arbitrary",), vmem_limit_bytes=VMEM_LIMIT),
        name="conv_ffn_sample",
    )(h, *weights, state_t)


def _rope_tables(pos):
    inv = ROPE_THETA ** (-jnp.arange(0, HEAD_DIM, 2, dtype=F32) / HEAD_DIM)
    ang = pos.astype(F32)[:, None] * inv[None, :]
    cos, sin = jnp.cos(ang), jnp.sin(ang)
    reps = LANES // HEAD_DIM
    return jnp.tile(jnp.concatenate([cos, cos], axis=-1), (1, reps)), jnp.tile(jnp.concatenate([-sin, sin], axis=-1), (1, reps))


def kernel(x_prompt, x_sample, cache_a_k, cache_a_v, state_c_conv, norm_mix_g, norm_ffn_g, w_ffn_up, w_ffn_down, w_in_ab, q_norm_g, k_norm_g, vb_norm_g, vb_norm_b, w_spatial, b_spatial, w_out_ab, w_c_in, w_c_dw, b_c_dw, c_norm_g, c_norm_b, w_c_out):
    batch, seq, _ = x_prompt.shape
    n_dec, dec_seq, _ = x_sample.shape
    depth = norm_mix_g.shape[0]
    n_ab, _, win_buf = cache_a_k.shape[:3]
    past_len = PAST_LEN
    assert dec_seq == 1 and seq % (BRANCHES[-1][1] * Q_BLOCK) == 0 and seq % CHUNK == 0
    tm = 256

    hp = x_prompt.reshape(batch * seq, D_MODEL)
    hs = x_sample.reshape(n_dec * dec_seq, D_MODEL)
    cos_p, sin_p = _rope_tables(jnp.arange(seq, dtype=jnp.int32))
    cos_s, sin_s = _rope_tables(jnp.full((n_dec,), past_len, dtype=jnp.int32))
    head_id = jnp.arange(LANES) // HEAD_DIM
    head_mean = jnp.where(head_id[:, None] == head_id[None, :], 1.0 / HEAD_DIM, 0.0).astype(BF16)
    cache_kt = cache_a_k.transpose(0, 1, 3, 4, 2)
    cache_vt = cache_a_v.transpose(0, 1, 3, 4, 2)
    state_t = state_c_conv.transpose(0, 2, 1, 3)
    group_dim = B_WIDTH // N_GROUPS
    row2 = lambda a: a.reshape(1, -1)

    w_in, wout = w_in_ab.astype(BF16), w_out_ab.astype(BF16)
    wc_in, wc_out = w_c_in.astype(BF16), w_c_out.astype(BF16)
    ffn_f32 = (w_ffn_up, w_ffn_down)
    ffn_bf = {}
    next_cast = lambda layer: (ffn_f32, layer + 1) if layer + 1 < depth else None

    ak_s, av_s, bv_p, bv_s, cc_p, cc_s = [], [], [], [], [], []
    kv_t = ()
    for layer in range(depth):
        j = layer // 2
        gffn = row2(norm_ffn_g[layer])
        gmix = row2(norm_mix_g[layer])
        if layer % 2 == 0:
            qg = row2(jnp.tile(q_norm_g[j], N_HEADS))
            kg = row2(jnp.tile(k_norm_g[j], N_HEADS))
            vbg, vbb = row2(vb_norm_g[j]), row2(vb_norm_b[j])
            qp, kp, vp, up, vbp, *rest = _ab_in(hp, gmix, w_in, j, cos_p, sin_p, qg, kg, head_mean, vbg, vbb, tm,
                                                seq=seq, prev_t=tuple(kv_t),
                                                cast=None if layer in ffn_bf else (ffn_f32, layer))
            kv_t = rest[:2]
            if layer not in ffn_bf:
                ffn_bf[layer] = tuple(rest[2:])
            wup, wdn = ffn_bf[layer]
            qs, ks, vs, us, vbs = _ab_in(hs, gmix, w_in, j, cos_s, sin_s, qg, kg, head_mean, vbg, vbb, n_dec)
            shp = (batch, seq, A_WIDTH)
            att_p, att_s = _attn(qp.reshape(shp), kp.reshape(shp), vp.reshape(shp), qs, ks, vs, cache_kt, cache_vt, j)
            att_p = att_p.reshape(batch * seq, A_WIDTH)
            w_pairs = w_spatial[j].reshape(N_GROUPS // 2, 2, CHUNK, CHUNK).transpose(0, 2, 1, 3).reshape(N_GROUPS // 2, CHUNK, 2 * CHUNK)
            b_rows = jnp.repeat(b_spatial[j].T, group_dim, axis=1)
            w0 = row2(jnp.repeat(w_spatial[j][:, 0, 0], group_dim))
            b0 = row2(jnp.repeat(b_spatial[j][:, 0], group_dim))
            hp, *cast_out = _mix_ffn(hp, att_p, up, vbp, w_pairs, b_rows, wout, j, gffn, wup, wdn, tm, False,
                                     cast=next_cast(layer))
            hs, = _mix_ffn(hs, att_s, us, vbs, w0, b0, wout, j, gffn, wup, wdn, n_dec, True)
            ak_s.append(ks.reshape(n_dec, dec_seq, N_HEADS, HEAD_DIM))
            av_s.append(vs.reshape(n_dec, dec_seq, N_HEADS, HEAD_DIM))
            last_chunk_start = ((seq - 1) // CHUNK) * CHUNK
            bv_p.append(vbp.reshape(batch, seq, B_WIDTH)[:, last_chunk_start:])
            bv_s.append(vbs.reshape(n_dec, dec_seq, B_WIDTH))
        else:
            wup, wdn = ffn_bf[layer]
            weights = (gmix, wc_in, w_c_dw[j], row2(b_c_dw[j]), row2(c_norm_g[j]),
                       row2(c_norm_b[j]), wc_out, gffn, wup, wdn)
            hp, new_cp, *cast_out = _conv_ffn_prompt(hp, weights, j, batch, seq, tm, cast=next_cast(layer))
            hs, new_cs = _conv_ffn_sample(hs, weights, j, state_t)
            cc_p.append(new_cp)
            cc_s.append(new_cs)
        if cast_out:
            ffn_bf[layer + 1] = tuple(cast_out)

    prompt_buf = min(BRANCHES[-1][0], seq)
    ak_p, av_p = [a.reshape(n_ab, batch, N_HEADS, HEAD_DIM, seq).transpose(0, 1, 4, 2, 3)[:, :, seq - prompt_buf:]
                  for a in kv_t]
    return (hp.reshape(batch, seq, D_MODEL), hs.reshape(n_dec, dec_seq, D_MODEL),
            ak_p, av_p, jnp.stack(ak_s), jnp.stack(av_s),
            jnp.stack(bv_p), jnp.stack(bv_s), jnp.stack(cc_p), jnp.stack(cc_s).transpose(0, 2, 1, 3))
```

```python
import functools

import jax
import jax.numpy as jnp
from jax import lax
from jax.experimental import pallas as pl
from jax.experimental.pallas import tpu as pltpu

F32 = jnp.float32
BF16 = jnp.bfloat16

D_MODEL = 1024
N_HEADS = 8
HEAD_DIM = 64
A_WIDTH = N_HEADS * HEAD_DIM
B_WIDTH = 512
N_GROUPS = 8
CHUNK = 128
Q_BLOCK = 128
BRANCHES = ((128, 1), (512, 4), (2048, 16))
ROPE_THETA = 10000.0
CONV_WIDTH = 31
PAST_LEN = 8192
D_FF = 4 * D_MODEL
EPS = 1e-6
IN_AB = 3 * A_WIDTH + 2 * B_WIDTH

LANES = 128
SUBLANES = 8
HALO = 32
CONV_ROWS = 128
FF_CHUNK = 1024
NEG = -1e30
LOG2_E = 1.4426950408889634
QK_SCALE = HEAD_DIM ** -0.5 * LOG2_E
ATTN_GROUP = 16
VMEM_LIMIT = 56 * 1024 * 1024


def _rmsnorm(x, g):
    ms = jnp.mean(x * x, axis=-1, keepdims=True)
    return x * lax.rsqrt(ms + EPS) * g


def _layernorm(x, g, b):
    mu = jnp.mean(x, axis=-1, keepdims=True)
    xc = x - mu
    var = jnp.mean(xc * xc, axis=-1, keepdims=True)
    return xc * lax.rsqrt(var + EPS) * g + b


def _dot(a, b):
    return jnp.dot(a, b, preferred_element_type=F32)


def _ffn(h1, g, wup_ref, wdn_ref):
    n = _rmsnorm(h1, g).astype(BF16)
    acc = h1
    for c in range(D_FF // FF_CHUNK):
        a = _dot(n, wup_ref[:, c * FF_CHUNK:(c + 1) * FF_CHUNK])
        a = jnp.maximum(a, 0.0)
        a = (a * a).astype(BF16)
        acc = acc + _dot(a, wdn_ref[c * FF_CHUNK:(c + 1) * FF_CHUNK, :])
    return acc


def _ab_in_kernel(h_ref, g_ref, w_ref, cos_ref, sin_ref, qg_ref, kg_ref, hm_ref, vbg_ref, vbb_ref, *refs,
                  n_prev, transposed):
    prev_refs, refs = refs[:2 if n_prev else 0], refs[2 if n_prev else 0:]
    q_ref, k_ref, v_ref, u_ref, vb_ref = refs[:5]
    kt_ref, vt_ref = refs[5:] if transposed else (None, None)
    if n_prev:
        kt_ref[0:n_prev] = prev_refs[0][...]
        vt_ref[0:n_prev] = prev_refs[1][...]
    n = _rmsnorm(h_ref[...], g_ref[...]).astype(BF16)
    cos = cos_ref[...]
    sin = sin_ref[...]
    hm = hm_ref[...]
    lane = lax.broadcasted_iota(jnp.int32, (1, LANES), 1)
    first_half = (lane & (HEAD_DIM // 2)) == 0

    def head_norm_rope(z, gain_ref, out_ref, scale, t_ref):
        for c in range(A_WIDTH // LANES):
            sl = slice(c * LANES, (c + 1) * LANES)
            zc = z[:, sl]
            sq = zc * zc
            hi = sq.astype(BF16)
            lo = (sq - hi.astype(F32)).astype(BF16)
            ms = _dot(hi, hm) + _dot(lo, hm)
            y = zc * lax.rsqrt(ms + EPS) * gain_ref[:, sl]
            partner = jnp.where(first_half, pltpu.roll(y, LANES - HEAD_DIM // 2, 1),
                                pltpu.roll(y, HEAD_DIM // 2, 1))
            out = y * cos + partner * sin
            if scale != 1.0:
                out = out * scale
            out_ref[:, sl] = out
            if t_ref is not None:
                t_ref[n_prev, sl, :] = out.T

    head_norm_rope(_dot(n, w_ref[:, 0:A_WIDTH]), qg_ref, q_ref, QK_SCALE, None)
    head_norm_rope(_dot(n, w_ref[:, A_WIDTH:2 * A_WIDTH]), kg_ref, k_ref, 1.0, kt_ref)
    v = _dot(n, w_ref[:, 2 * A_WIDTH:3 * A_WIDTH])
    v_ref[...] = v
    if transposed:
        for c in range(A_WIDTH // LANES):
            sl = slice(c * LANES, (c + 1) * LANES)
            vt_ref[n_prev, sl, :] = v[:, sl].T
    u_ref[...] = jax.nn.gelu(_dot(n, w_ref[:, 3 * A_WIDTH:3 * A_WIDTH + B_WIDTH]))
    vb = jax.nn.gelu(_dot(n, w_ref[:, 3 * A_WIDTH + B_WIDTH:IN_AB]))
    vb_ref[...] = _layernorm(vb, vbg_ref[...], vbb_ref[...])


def _const_spec(shape):
    nd = len(shape)
    return pl.BlockSpec(shape, lambda *_: (0,) * nd, pipeline_mode=pl.Buffered(1))


def _layer_spec(shape, layer):
    nd = len(shape)
    return pl.BlockSpec((None, *shape), lambda *_: (layer,) + (0,) * nd, pipeline_mode=pl.Buffered(1))


def _with_cast_duty(body, n_in, n_out):
    def kernel(*refs):
        ins, srcs = refs[:n_in], refs[n_in:n_in + 2]
        outs, dsts = refs[n_in + 2:n_in + 2 + n_out], refs[n_in + 2 + n_out:n_in + 4 + n_out]
        for src, dst in zip(srcs, dsts):
            dst[...] = src[...].astype(BF16)
        body(*ins, *outs, *refs[n_in + 4 + n_out:])
    return kernel


def _cast_duty_specs(stacks, layer, n_steps, step_of):
    in_specs, out_specs, out_shape = [], [], []
    for w in stacks:
        _, rows, cols = w.shape
        per_step = rows // n_steps
        assert per_step * n_steps == rows and per_step % 16 == 0
        in_specs.append(pl.BlockSpec((None, per_step, cols), lambda *ids: (layer, step_of(*ids), 0)))
        out_specs.append(pl.BlockSpec((per_step, cols), lambda *ids: (step_of(*ids), 0)))
        out_shape.append(jax.ShapeDtypeStruct((rows, cols), BF16))
    return in_specs, out_specs, out_shape


def _ab_in(h, g, w_bf, j, cos_t, sin_t, qg, kg, hm, vbg, vbb, tm, seq=None, prev_t=(), cast=None):
    t = h.shape[0]
    n_pos_blocks = cos_t.shape[0] // tm
    row = lambda i: (i, 0)
    tab = lambda i: (i % n_pos_blocks, 0)
    out = jax.ShapeDtypeStruct((t, A_WIDTH), F32)
    in_specs = [
        pl.BlockSpec((tm, D_MODEL), row), _const_spec((1, D_MODEL)), _layer_spec((D_MODEL, IN_AB), j),
        pl.BlockSpec((tm, LANES), tab), pl.BlockSpec((tm, LANES), tab),
        _const_spec((1, A_WIDTH)), _const_spec((1, A_WIDTH)), _const_spec((LANES, LANES)),
        _const_spec((1, B_WIDTH)), _const_spec((1, B_WIDTH)),
    ]
    out_specs = [pl.BlockSpec((tm, A_WIDTH), row)] * 5
    out_shape = [out] * 5
    n_prev = prev_t[0].shape[0] if prev_t else 0
    if seq is not None:
        tiles_per_seq = seq // tm
        t_map = lambda i: (0, i // tiles_per_seq, 0, i % tiles_per_seq)
        if prev_t:
            in_specs += [pl.BlockSpec((n_prev, None, A_WIDTH, tm), t_map)] * 2
        out_specs += [pl.BlockSpec((n_prev + 1, None, A_WIDTH, tm), t_map)] * 2
        out_shape += [jax.ShapeDtypeStruct((n_prev + 1, t // seq, A_WIDTH, seq), F32)] * 2
    body = functools.partial(_ab_in_kernel, n_prev=n_prev, transposed=seq is not None)
    operands = [h, g, w_bf, cos_t, sin_t, qg, kg, hm, vbg, vbb, *prev_t]
    if cast is not None:
        stacks, cast_layer = cast
        body = _with_cast_duty(body, len(in_specs), len(out_specs))
        c_in, c_out, c_shape = _cast_duty_specs(stacks, cast_layer, t // tm, lambda i: i)
        in_specs, out_specs, out_shape = in_specs + c_in, out_specs + c_out, out_shape + c_shape
        operands += list(stacks)
    return pl.pallas_call(
        body,
        grid=(t // tm,),
        in_specs=in_specs,
        out_specs=out_specs,
        out_shape=out_shape,
        compiler_params=pltpu.CompilerParams(dimension_semantics=("arbitrary",), vmem_limit_bytes=VMEM_LIMIT),
        name="ab_in",
    )(*operands)


def _attn_prompt_kernel(q_ref, k_ref, v_ref, o_ref, acc_scr, m_scr, den_scr, *, seq):
    lane = lax.broadcasted_iota(jnp.int32, (1, LANES), 1)
    head0 = lane < HEAD_DIM
    ri = lax.broadcasted_iota(jnp.int32, (2 * Q_BLOCK, 2 * Q_BLOCK), 0) & (Q_BLOCK - 1)
    ci = lax.broadcasted_iota(jnp.int32, (2 * Q_BLOCK, 2 * Q_BLOCK), 1)
    band_two = jnp.where(ci < Q_BLOCK, ci - ri, ri + Q_BLOCK - ci) >= 0
    band_one = (lax.broadcasted_iota(jnp.int32, (2 * Q_BLOCK, Q_BLOCK), 1)
                <= lax.broadcasted_iota(jnp.int32, (2 * Q_BLOCK, Q_BLOCK), 0) & (Q_BLOCK - 1))

    def rows(start, dil):
        if dil == 1:
            return pl.ds(start if isinstance(start, int) else pl.multiple_of(start, Q_BLOCK), Q_BLOCK)
        return pl.ds(start, Q_BLOCK, stride=dil)

    def load(dil, start, with_prev):
        cur = rows(start, dil)
        q = q_ref[0, cur, :]
        q2 = jnp.concatenate([jnp.where(head0, q, 0.0), jnp.where(head0, 0.0, q)], axis=0).astype(BF16)
        k = k_ref[0, cur, :].astype(BF16)
        v = v_ref[0, cur, :].astype(BF16)
        if with_prev:
            prev = rows(start - dil * Q_BLOCK, dil)
            k = jnp.concatenate([k_ref[0, prev, :].astype(BF16), k], axis=0)
            v = jnp.concatenate([v_ref[0, prev, :].astype(BF16), v], axis=0)
        return q2, k, v

    def attend(q2, k, v, with_prev):
        n_keys = k.shape[0]
        s = lax.dot_general(q2, k, (((1,), (1,)), ((), ())), preferred_element_type=F32)
        s = jnp.where(band_two if with_prev else band_one, s, NEG)
        m = jnp.max(s, axis=-1, keepdims=True)
        p = jnp.exp2(s - m).astype(BF16)
        r = _dot(p, jnp.concatenate([v, jnp.ones((n_keys, LANES), BF16)], axis=1))
        acc, den = r[:, :LANES], r[:, LANES:]
        m = jnp.broadcast_to(m, den.shape)
        return [jnp.where(head0, x[:Q_BLOCK], x[Q_BLOCK:]) for x in (acc, m, den)]

    def blocks(branch, dil, starts, with_prev):
        loaded = [load(dil, st, with_prev) for st in starts]
        results = [attend(*ld, with_prev) for ld in loaded]
        for st, vals in zip(starts, results):
            for scr, val in zip((acc_scr, m_scr, den_scr), vals):
                scr[branch, rows(st, dil), :] = val

    def grouped_loop(total, body):
        n_full = total // ATTN_GROUP

        def loop_body(it, carry):
            body([it * ATTN_GROUP + g for g in range(ATTN_GROUP)])
            return carry

        if n_full:
            lax.fori_loop(0, n_full, loop_body, 0)
        if total % ATTN_GROUP:
            body(list(range(n_full * ATTN_GROUP, total)))

    for branch, (window, dil) in enumerate(BRANCHES):
        assert window // dil == Q_BLOCK
        sub_len = seq // dil
        n_blocks = sub_len // Q_BLOCK

        def first_body(items, branch=branch, dil=dil):
            blocks(branch, dil, items, False)

        grouped_loop(dil, first_body)
        if n_blocks > 1:
            def rest_body(items, branch=branch, dil=dil, n_blocks=n_blocks):
                starts = [it // (n_blocks - 1) + dil * Q_BLOCK * (it % (n_blocks - 1) + 1) for it in items]
                blocks(branch, dil, starts, True)

            grouped_loop(dil * (n_blocks - 1), rest_body)

    rows_per_step = 256

    def merge_body(it, carry):
        sl = pl.ds(pl.multiple_of(it * rows_per_step, rows_per_step), rows_per_step)
        ms = [m_scr[b, sl, :] for b in range(len(BRANCHES))]
        m = functools.reduce(jnp.maximum, ms)
        ws = [jnp.exp2(mb - m) for mb in ms]
        num = sum(w * acc_scr[b, sl, :] for b, w in enumerate(ws))
        den = sum(w * den_scr[b, sl, :] for b, w in enumerate(ws))
        o_ref[0, sl, :] = (num * (1.0 / den)).astype(o_ref.dtype)
        return carry

    lax.fori_loop(0, seq // rows_per_step, merge_body, 0)


def _attn(q, k, v, q_s, k_s, v_s, cache_kt, cache_vt, layer):
    b, seq, _ = q.shape
    n_slabs = A_WIDTH // LANES
    n, win_buf = cache_kt.shape[1], cache_kt.shape[-1]
    assert n == b * n_slabs
    for window, dil in BRANCHES:
        assert window <= win_buf <= PAST_LEN and dil & (dil - 1) == 0
    spec = pl.BlockSpec((1, seq, LANES), lambda i, j: (i, 0, j))
    rows = pl.BlockSpec((n, A_WIDTH), lambda i, j: (0, 0))
    cache = pl.BlockSpec((None, None, N_HEADS, HEAD_DIM, win_buf), lambda i, j: (layer, i * n_slabs + j, 0, 0, 0))
    prompt_scratch = [pltpu.VMEM((len(BRANCHES), seq, LANES), F32)] * 3
    sample_scratch = [pltpu.VMEM((N_HEADS, win_buf), F32), pltpu.VMEM((N_HEADS, LANES), F32),
                      pltpu.VMEM((N_HEADS, win_buf), F32)]

    def kernel(*refs):
        prompt_in, sample_in, (o_ref, os_ref), scratch = refs[:3], refs[3:8], refs[8:10], refs[10:]
        step = pl.program_id(0) * n_slabs + pl.program_id(1)
        _attn_sample_kernel(step, *sample_in, os_ref, *scratch[len(prompt_scratch):], win_buf=win_buf)
        _attn_prompt_kernel(*prompt_in, o_ref, *scratch[:len(prompt_scratch)], seq=seq)

    return pl.pallas_call(
        kernel,
        grid=(b, n_slabs),
        in_specs=[spec, spec, spec, rows, rows, rows, cache, cache],
        out_specs=[spec, rows],
        out_shape=[jax.ShapeDtypeStruct((b, seq, A_WIDTH), BF16), jax.ShapeDtypeStruct((n, A_WIDTH), F32)],
        scratch_shapes=prompt_scratch + sample_scratch,
        compiler_params=pltpu.CompilerParams(dimension_semantics=("arbitrary", "arbitrary"),
                                             vmem_limit_bytes=VMEM_LIMIT),
        name="attn",
    )(q, k, v, q_s, k_s, v_s, cache_kt, cache_vt)


def _attn_sample_kernel(n, q_ref, kn_ref, vn_ref, kt_ref, vt_ref, o_ref, s_scr, sn_scr, p_scr, *, win_buf):
    heads_per_slab = LANES // HEAD_DIM

    def head_columns(ref):
        r = ref[pl.ds(n, 1), :]
        slabs = [jnp.broadcast_to(r[:, c * LANES:(c + 1) * LANES], (LANES, LANES)).T for c in range(A_WIDTH // LANES)]
        return [slabs[h // heads_per_slab][(h % heads_per_slab) * HEAD_DIM:(h % heads_per_slab + 1) * HEAD_DIM, 0:1]
                for h in range(N_HEADS)]

    q_cols, kn_cols, vn_cols = head_columns(q_ref), head_columns(kn_ref), head_columns(vn_ref)
    back = win_buf - lax.broadcasted_iota(jnp.int32, (1, win_buf), 1)
    masks = [jnp.where((back & (dil - 1)) == 0, back, window + 1) <= window for window, dil in BRANCHES]
    for h in range(N_HEADS):
        q = q_cols[h]
        s_scr[h:h + 1, :] = jnp.sum(kt_ref[h] * q, axis=0, keepdims=True)
        sn_scr[h:h + 1, :] = jnp.broadcast_to(jnp.sum(q * kn_cols[h], axis=0, keepdims=True), (1, LANES))
    s = s_scr[...]
    s_new = sn_scr[:, 0:1]
    ps, pns, lses = [], [], []
    for mask in masks:
        sb = jnp.where(mask, s, NEG)
        m = jnp.maximum(jnp.max(sb, axis=1, keepdims=True), s_new)
        p = jnp.exp2(sb - m)
        pn = jnp.exp2(s_new - m)
        den = jnp.sum(p, axis=1, keepdims=True) + pn
        inv = 1.0 / den
        ps.append(p * inv)
        pns.append(pn * inv)
        lses.append(m + jnp.log2(den))
    mm = functools.reduce(jnp.maximum, lses)
    ws = [jnp.exp2(l - mm) for l in lses]
    inv = 1.0 / sum(ws)
    p_scr[...] = sum(w * p for w, p in zip(ws, ps)) * inv
    pn_all = sum(w * pn for w, pn in zip(ws, pns)) * inv
    o_cols = [jnp.sum(vt_ref[h] * p_scr[h:h + 1, :], axis=1, keepdims=True) + pn_all[h:h + 1] * vn_cols[h]
              for h in range(N_HEADS)]
    out = []
    for c in range(A_WIDTH // LANES):
        col = jnp.concatenate(o_cols[c * heads_per_slab:(c + 1) * heads_per_slab], axis=0)
        out.append(jnp.broadcast_to(col, (LANES, LANES)).T[0:1, :])
    o_ref[pl.ds(n, 1), :] = jnp.concatenate(out, axis=1)


def _mix_ffn_prompt_kernel(h_ref, att_ref, u_ref, vb_ref, wsp_ref, bsp_ref, wout_ref, gffn_ref, wup_ref, wdn_ref,
                           o_ref, gate_scr, *, tm):
    lane = lax.broadcasted_iota(jnp.int32, (1, LANES), 1)
    lo = lane < B_WIDTH // N_GROUPS
    ri = lax.broadcasted_iota(jnp.int32, (CHUNK, 2 * CHUNK), 0)
    ci = lax.broadcasted_iota(jnp.int32, (CHUNK, 2 * CHUNK), 1)
    causal = (ci & (CHUNK - 1)) <= ri
    n_pairs = B_WIDTH // LANES
    w_pairs = [jnp.where(causal, wsp_ref[p], 0.0).astype(BF16) for p in range(n_pairs)]
    for c in range(tm // CHUNK):
        rows = slice(c * CHUNK, (c + 1) * CHUNK)
        for p in range(n_pairs):
            sl = slice(p * LANES, (p + 1) * LANES)
            slab = vb_ref[rows, sl]
            rhs = jnp.concatenate([jnp.where(lo, slab, 0.0), jnp.where(lo, 0.0, slab)], axis=0).astype(BF16)
            mixed = _dot(w_pairs[p], rhs) + bsp_ref[:, sl]
            gate_scr[rows, sl] = (u_ref[rows, sl] * mixed).astype(BF16)
    mix = _dot(att_ref[...], wout_ref[0:A_WIDTH, :]) + _dot(gate_scr[...], wout_ref[A_WIDTH:, :])
    h1 = h_ref[...] + mix
    o_ref[...] = _ffn(h1, gffn_ref[...], wup_ref, wdn_ref)


def _mix_ffn_sample_kernel(h_ref, att_ref, u_ref, vb_ref, w0_ref, b0_ref, wout_ref, gffn_ref, wup_ref, wdn_ref,
                           o_ref):
    gate = (u_ref[...] * (vb_ref[...] * w0_ref[...] + b0_ref[...])).astype(BF16)
    mix = _dot(att_ref[...].astype(BF16), wout_ref[0:A_WIDTH, :]) + _dot(gate, wout_ref[A_WIDTH:, :])
    h1 = h_ref[...] + mix
    o_ref[...] = _ffn(h1, gffn_ref[...], wup_ref, wdn_ref)


def _mix_ffn(h, att, u, vb, gate_w, gate_b, wout_bf, j, gffn, wup_bf, wdn_bf, tm, sample, cast=None):
    t = h.shape[0]
    row = lambda i: (i, 0)
    if sample:
        body = _mix_ffn_sample_kernel
        scratch = []
    else:
        body = functools.partial(_mix_ffn_prompt_kernel, tm=tm)
        scratch = [pltpu.VMEM((tm, B_WIDTH), BF16)]
    in_specs = [
        pl.BlockSpec((tm, D_MODEL), row), pl.BlockSpec((tm, A_WIDTH), row),
        pl.BlockSpec((tm, B_WIDTH), row), pl.BlockSpec((tm, B_WIDTH), row),
        _const_spec(gate_w.shape), _const_spec(gate_b.shape),
        _layer_spec((A_WIDTH + B_WIDTH, D_MODEL), j), _const_spec((1, D_MODEL)),
        _const_spec((D_MODEL, D_FF)), _const_spec((D_FF, D_MODEL)),
    ]
    out_specs = [pl.BlockSpec((tm, D_MODEL), row)]
    out_shape = [jax.ShapeDtypeStruct((t, D_MODEL), F32)]
    operands = [h, att, u, vb, gate_w, gate_b, wout_bf, gffn, wup_bf, wdn_bf]
    if cast is not None:
        stacks, cast_layer = cast
        body = _with_cast_duty(body, len(in_specs), len(out_specs))
        c_in, c_out, c_shape = _cast_duty_specs(stacks, cast_layer, t // tm, lambda i: i)
        in_specs, out_specs, out_shape = in_specs + c_in, out_specs + c_out, out_shape + c_shape
        operands += list(stacks)
    return pl.pallas_call(
        body,
        grid=(t // tm,),
        in_specs=in_specs,
        out_specs=out_specs,
        out_shape=out_shape,
        scratch_shapes=scratch,
        compiler_params=pltpu.CompilerParams(dimension_semantics=("arbitrary",), vmem_limit_bytes=VMEM_LIMIT),
        name="mix_ffn_sample" if sample else "mix_ffn",
    )(*operands)


def _glu_in(h, gmix, win_ref):
    n = _rmsnorm(h, gmix).astype(BF16)
    a = _dot(n, win_ref[:, 0:D_MODEL])
    gate = _dot(n, win_ref[:, D_MODEL:])
    return a * jax.nn.sigmoid(gate)


def _conv_tail(h, y, cg_ref, cb_ref, wout_ref, gffn_ref, wup_ref, wdn_ref):
    z = _layernorm(y, cg_ref[...], cb_ref[...])
    z = (z * jax.nn.sigmoid(z)).astype(BF16)
    h1 = h + _dot(z, wout_ref[...])
    return _ffn(h1, gffn_ref[...], wup_ref, wdn_ref)


def _conv_ffn_prompt_kernel(h_ref, gmix_ref, win_ref, wdw_ref, bdw_ref, cg_ref, cb_ref, wout_ref, gffn_ref,
                            wup_ref, wdn_ref, o_ref, cc_ref, xbuf, xs_scr, y_scr, h1_scr, *, tm, tiles_per_seq):
    g = pl.program_id(0)
    keep = CONV_WIDTH - 1
    n_slabs = D_MODEL // LANES
    n_chunks = D_FF // FF_CHUNK

    @pl.when(g == 0)
    def _():
        h1_scr[...] = jnp.zeros(h1_scr.shape, F32)
        xbuf[0:HALO, :] = jnp.zeros((HALO, D_MODEL), F32)

    s = jnp.minimum(g, pl.num_programs(0) - 2) % tiles_per_seq
    xbuf[0:HALO, :] = jnp.where(s == 0, 0.0, xbuf[0:HALO, :])
    h = h_ref[...]
    xbuf[HALO:HALO + tm, :] = _glu_in(h, gmix_ref[...], win_ref)

    shift_rows = tm + HALO - SUBLANES
    for b in range(1, SUBLANES):
        xs_scr[b - 1] = xbuf[b:b + shift_rows, :]

    def conv_slab(col, anchor):
        cs = slice(col * LANES, (col + 1) * LANES)
        taps = [wdw_ref[w:w + 1, cs] for w in range(CONV_WIDTH)]
        bias = jnp.broadcast_to(bdw_ref[:, cs], (CONV_ROWS, LANES))
        for r0 in range(0, tm, CONV_ROWS):
            acc = jnp.where(g >= 0, bias, anchor)
            for w in range(CONV_WIDTH):
                a, b = divmod(HALO - keep + w, SUBLANES)
                rows = slice(r0 + SUBLANES * a, r0 + SUBLANES * a + CONV_ROWS)
                x = xbuf[rows, cs] if b == 0 else xs_scr[b - 1, rows, cs]
                acc = acc + x * jnp.where(g >= 0, taps[w], anchor[0:1])
            y_scr[r0:r0 + CONV_ROWS, cs] = acc
            anchor = acc
        return anchor

    h1 = h1_scr[(g + 1) % 2]
    n = _rmsnorm(h1, gffn_ref[...]).astype(BF16)
    acc = h1
    slabs_per_chunk = n_slabs // n_chunks
    for c in range(n_chunks):
        a = jnp.maximum(_dot(n, wup_ref[:, c * FF_CHUNK:(c + 1) * FF_CHUNK]), 0.0)
        anchor = a[0:CONV_ROWS, 0:LANES]
        for i in range(slabs_per_chunk):
            anchor = conv_slab(c * slabs_per_chunk + i, anchor)
        acc = acc + _dot((a * a).astype(BF16), wdn_ref[c * FF_CHUNK:(c + 1) * FF_CHUNK, :])
    o_ref[...] = acc

    cc_ref[0] = xbuf[HALO + tm - keep:HALO + tm, :]
    xbuf[0:HALO, :] = xbuf[tm:tm + HALO, :]
    z = _layernorm(y_scr[...], cg_ref[...], cb_ref[...])
    z = (z * jax.nn.sigmoid(z)).astype(BF16)
    h1_scr[g % 2] = h + _dot(z, wout_ref[...])


def _conv_ffn_sample_kernel(h_ref, gmix_ref, win_ref, wdw_ref, bdw_ref, cg_ref, cb_ref, wout_ref, gffn_ref,
                            wup_ref, wdn_ref, st_ref, o_ref, cc_ref):
    keep = CONV_WIDTH - 1
    h = h_ref[...]
    x = _glu_in(h, gmix_ref[...], win_ref)
    y = x * wdw_ref[keep:keep + 1, :] + bdw_ref[...]
    for w in range(keep):
        y = y + st_ref[w] * wdw_ref[w:w + 1, :]
    for w in range(keep - 1):
        cc_ref[w] = st_ref[w + 1]
    cc_ref[keep - 1] = x
    o_ref[...] = _conv_tail(h, y, cg_ref, cb_ref, wout_ref, gffn_ref, wup_ref, wdn_ref)


def _conv_weight_specs(j):
    return [
        _const_spec((1, D_MODEL)), _layer_spec((D_MODEL, 2 * D_MODEL), j), _const_spec((CONV_WIDTH, D_MODEL)),
        _const_spec((1, D_MODEL)), _const_spec((1, D_MODEL)), _const_spec((1, D_MODEL)),
        _layer_spec((D_MODEL, D_MODEL), j), _const_spec((1, D_MODEL)),
        _const_spec((D_MODEL, D_FF)), _const_spec((D_FF, D_MODEL)),
    ]


def _conv_ffn_prompt(h, weights, j, batch, seq, tm, cast=None):
    keep = CONV_WIDTH - 1
    n_s = seq // tm
    n_tiles = batch * n_s
    conv_tile = lambda g: (jnp.minimum(g, n_tiles - 1), 0)
    ffn_tile = lambda g: (jnp.maximum(g - 1, 0), 0)
    body = functools.partial(_conv_ffn_prompt_kernel, tm=tm, tiles_per_seq=n_s)
    in_specs = [pl.BlockSpec((tm, D_MODEL), conv_tile)] + _conv_weight_specs(j)
    out_specs = [pl.BlockSpec((tm, D_MODEL), ffn_tile),
                 pl.BlockSpec((1, keep, D_MODEL), lambda g: (jnp.minimum(g, n_tiles - 1) // n_s, 0, 0))]
    out_shape = [jax.ShapeDtypeStruct((batch * seq, D_MODEL), F32),
                 jax.ShapeDtypeStruct((batch, keep, D_MODEL), F32)]
    operands = [h, *weights]
    if cast is not None:
        stacks, cast_layer = cast
        body = _with_cast_duty(body, len(in_specs), len(out_specs))
        c_in, c_out, c_shape = _cast_duty_specs(stacks, cast_layer, n_tiles, lambda g: jnp.minimum(g, n_tiles - 1))
        in_specs, out_specs, out_shape = in_specs + c_in, out_specs + c_out, out_shape + c_shape
        operands += list(stacks)
    return pl.pallas_call(
        body,
        grid=(n_tiles + 1,),
        in_specs=in_specs,
        out_specs=out_specs,
        out_shape=out_shape,
        scratch_shapes=[pltpu.VMEM((HALO + tm, D_MODEL), F32),
                        pltpu.VMEM((SUBLANES - 1, HALO + tm - SUBLANES, D_MODEL), F32),
                        pltpu.VMEM((tm, D_MODEL), F32),
                        pltpu.VMEM((2, tm, D_MODEL), F32)],
        compiler_params=pltpu.CompilerParams(dimension_semantics=("arbitrary",), vmem_limit_bytes=VMEM_LIMIT),
        name="conv_ffn",
    )(*operands)


def _conv_ffn_sample(h, weights, j, state_t):
    n = h.shape[0]
    keep = CONV_WIDTH - 1
    return pl.pallas_call(
        _conv_ffn_sample_kernel,
        grid=(1,),
        in_specs=[_const_spec((n, D_MODEL))] + _conv_weight_specs(j)
        + [pl.BlockSpec((None, keep, n, D_MODEL), lambda i: (j, 0, 0, 0))],
        out_specs=[_const_spec((n, D_MODEL)), _const_spec((keep, n, D_MODEL))],
        out_shape=[jax.ShapeDtypeStruct((n, D_MODEL), F32), jax.ShapeDtypeStruct((keep, n, D_MODEL), F32)],
        compiler_params=pltpu.CompilerParams(dimension_semantics=("arbitrary",), vmem_limit_bytes=VMEM_LIMIT),
        name="conv_ffn_sample",
    )(h, *weights, state_t)


def _rope_tables(pos):
    inv = ROPE_THETA ** (-jnp.arange(0, HEAD_DIM, 2, dtype=F32) / HEAD_DIM)
    ang = pos.astype(F32)[:, None] * inv[None, :]
    cos, sin = jnp.cos(ang), jnp.sin(ang)
    reps = LANES // HEAD_DIM
    return jnp.tile(jnp.concatenate([cos, cos], axis=-1), (1, reps)), jnp.tile(jnp.concatenate([-sin, sin], axis=-1), (1, reps))


def kernel(x_prompt, x_sample, cache_a_k, cache_a_v, state_c_conv, norm_mix_g, norm_ffn_g, w_ffn_up, w_ffn_down, w_in_ab, q_norm_g, k_norm_g, vb_norm_g, vb_norm_b, w_spatial, b_spatial, w_out_ab, w_c_in, w_c_dw, b_c_dw, c_norm_g, c_norm_b, w_c_out):
    batch, seq, _ = x_prompt.shape
    n_dec, dec_seq, _ = x_sample.shape
    depth = norm_mix_g.shape[0]
    n_ab, _, win_buf = cache_a_k.shape[:3]
    past_len = PAST_LEN
    assert dec_seq == 1 and seq % (BRANCHES[-1][1] * Q_BLOCK) == 0 and seq % CHUNK == 0
    tm = 256

    hp = x_prompt.reshape(batch * seq, D_MODEL)
    hs = x_sample.reshape(n_dec * dec_seq, D_MODEL)
    cos_p, sin_p = _rope_tables(jnp.arange(seq, dtype=jnp.int32))
    cos_s, sin_s = _rope_tables(jnp.full((n_dec,), past_len, dtype=jnp.int32))
    head_id = jnp.arange(LANES) // HEAD_DIM
    head_mean = jnp.where(head_id[:, None] == head_id[None, :], 1.0 / HEAD_DIM, 0.0).astype(BF16)
    cache_kt = cache_a_k.transpose(0, 1, 3, 4, 2)
    cache_vt = cache_a_v.transpose(0, 1, 3, 4, 2)
    state_t = state_c_conv.transpose(0, 2, 1, 3)
    group_dim = B_WIDTH // N_GROUPS
    row2 = lambda a: a.reshape(1, -1)

    w_in, wout = w_in_ab.astype(BF16), w_out_ab.astype(BF16)
    wc_in, wc_out = w_c_in.astype(BF16), w_c_out.astype(BF16)
    ffn_f32 = (w_ffn_up, w_ffn_down)
    ffn_bf = {}
    next_cast = lambda layer: (ffn_f32, layer + 1) if layer + 1 < depth else None

    ak_s, av_s, bv_p, bv_s, cc_p, cc_s = [], [], [], [], [], []
    kv_t = ()
    for layer in range(depth):
        j = layer // 2
        gffn = row2(norm_ffn_g[layer])
        gmix = row2(norm_mix_g[layer])
        if layer % 2 == 0:
            qg = row2(jnp.tile(q_norm_g[j], N_HEADS))
            kg = row2(jnp.tile(k_norm_g[j], N_HEADS))
            vbg, vbb = row2(vb_norm_g[j]), row2(vb_norm_b[j])
            qp, kp, vp, up, vbp, *rest = _ab_in(hp, gmix, w_in, j, cos_p, sin_p, qg, kg, head_mean, vbg, vbb, 2 * tm,
                                                seq=seq, prev_t=tuple(kv_t),
                                                cast=None if layer in ffn_bf else (ffn_f32, layer))
            kv_t = rest[:2]
            if layer not in ffn_bf:
                ffn_bf[layer] = tuple(rest[2:])
            wup, wdn = ffn_bf[layer]
            qs, ks, vs, us, vbs = _ab_in(hs, gmix, w_in, j, cos_s, sin_s, qg, kg, head_mean, vbg, vbb, n_dec)
            shp = (batch, seq, A_WIDTH)
            att_p, att_s = _attn(qp.reshape(shp), kp.reshape(shp), vp.reshape(shp), qs, ks, vs, cache_kt, cache_vt, j)
            att_p = att_p.reshape(batch * seq, A_WIDTH)
            w_pairs = w_spatial[j].reshape(N_GROUPS // 2, 2, CHUNK, CHUNK).transpose(0, 2, 1, 3).reshape(N_GROUPS // 2, CHUNK, 2 * CHUNK)
            b_rows = jnp.repeat(b_spatial[j].T, group_dim, axis=1)
            w0 = row2(jnp.repeat(w_spatial[j][:, 0, 0], group_dim))
            b0 = row2(jnp.repeat(b_spatial[j][:, 0], group_dim))
            hp, *cast_out = _mix_ffn(hp, att_p, up, vbp, w_pairs, b_rows, wout, j, gffn, wup, wdn, tm, False,
                                     cast=next_cast(layer))
            hs, = _mix_ffn(hs, att_s, us, vbs, w0, b0, wout, j, gffn, wup, wdn, n_dec, True)
            ak_s.append(ks.reshape(n_dec, dec_seq, N_HEADS, HEAD_DIM))
            av_s.append(vs.reshape(n_dec, dec_seq, N_HEADS, HEAD_DIM))
            last_chunk_start = ((seq - 1) // CHUNK) * CHUNK
            bv_p.append(vbp.reshape(batch, seq, B_WIDTH)[:, last_chunk_start:])
            bv_s.append(vbs.reshape(n_dec, dec_seq, B_WIDTH))
        else:
            wup, wdn = ffn_bf[layer]
            weights = (gmix, wc_in, w_c_dw[j], row2(b_c_dw[j]), row2(c_norm_g[j]),
                       row2(c_norm_b[j]), wc_out, gffn, wup, wdn)
            hp, new_cp, *cast_out = _conv_ffn_prompt(hp, weights, j, batch, seq, tm, cast=next_cast(layer))
            hs, new_cs = _conv_ffn_sample(hs, weights, j, state_t)
            cc_p.append(new_cp)
            cc_s.append(new_cs)
        if cast_out:
            ffn_bf[layer + 1] = tuple(cast_out)

    prompt_buf = min(BRANCHES[-1][0], seq)
    ak_p, av_p = [a.reshape(n_ab, batch, N_HEADS, HEAD_DIM, seq).transpose(0, 1, 4, 2, 3)[:, :, seq - prompt_buf:]
                  for a in kv_t]
    return (hp.reshape(batch, seq, D_MODEL), hs.reshape(n_dec, dec_seq, D_MODEL),
            ak_p, av_p, jnp.stack(ak_s), jnp.stack(av_s),
            jnp.stack(bv_p), jnp.stack(bv_s), jnp.stack(cc_p), jnp.stack(cc_s).transpose(0, 2, 1, 3))
```

```python
import functools

import jax
import jax.numpy as jnp
from jax import lax
from jax.experimental import pallas as pl
from jax.experimental.pallas import tpu as pltpu

F32 = jnp.float32
BF16 = jnp.bfloat16

D_MODEL = 1024
N_HEADS = 8
HEAD_DIM = 64
A_WIDTH = N_HEADS * HEAD_DIM
B_WIDTH = 512
N_GROUPS = 8
CHUNK = 128
Q_BLOCK = 128
BRANCHES = ((128, 1), (512, 4), (2048, 16))
ROPE_THETA = 10000.0
CONV_WIDTH = 31
PAST_LEN = 8192
D_FF = 4 * D_MODEL
EPS = 1e-6
IN_AB = 3 * A_WIDTH + 2 * B_WIDTH

LANES = 128
SUBLANES = 8
HALO = 32
CONV_ROWS = 128
FF_CHUNK = 1024
NEG = -1e30
LOG2_E = 1.4426950408889634
QK_SCALE = HEAD_DIM ** -0.5 * LOG2_E
ATTN_GROUP = 16
VMEM_LIMIT = 56 * 1024 * 1024


def _rmsnorm(x, g):
    ms = jnp.mean(x * x, axis=-1, keepdims=True)
    return x * lax.rsqrt(ms + EPS) * g


def _layernorm(x, g, b):
    mu = jnp.mean(x, axis=-1, keepdims=True)
    xc = x - mu
    var = jnp.mean(xc * xc, axis=-1, keepdims=True)
    return xc * lax.rsqrt(var + EPS) * g + b


def _dot(a, b):
    return jnp.dot(a, b, preferred_element_type=F32)


def _ffn(h1, g, wup_ref, wdn_ref):
    n = _rmsnorm(h1, g).astype(BF16)
    acc = h1
    for c in range(D_FF // FF_CHUNK):
        a = _dot(n, wup_ref[:, c * FF_CHUNK:(c + 1) * FF_CHUNK])
        a = jnp.maximum(a, 0.0)
        a = (a * a).astype(BF16)
        acc = acc + _dot(a, wdn_ref[c * FF_CHUNK:(c + 1) * FF_CHUNK, :])
    return acc


def _ab_in_kernel(h_ref, g_ref, w_ref, cos_ref, sin_ref, qg_ref, kg_ref, hm_ref, vbg_ref, vbb_ref, *refs,
                  n_prev, transposed):
    prev_refs, refs = refs[:2 if n_prev else 0], refs[2 if n_prev else 0:]
    q_ref, k_ref, v_ref, u_ref, vb_ref = refs[:5]
    kt_ref, vt_ref = refs[5:] if transposed else (None, None)
    if n_prev:
        kt_ref[0:n_prev] = prev_refs[0][...]
        vt_ref[0:n_prev] = prev_refs[1][...]
    n = _rmsnorm(h_ref[...], g_ref[...]).astype(BF16)
    cos = cos_ref[...]
    sin = sin_ref[...]
    hm = hm_ref[...]
    lane = lax.broadcasted_iota(jnp.int32, (1, LANES), 1)
    first_half = (lane & (HEAD_DIM // 2)) == 0

    def head_norm_rope(z, gain_ref, out_ref, scale, t_ref):
        for c in range(A_WIDTH // LANES):
            sl = slice(c * LANES, (c + 1) * LANES)
            zc = z[:, sl]
            sq = zc * zc
            hi = sq.astype(BF16)
            lo = (sq - hi.astype(F32)).astype(BF16)
            ms = _dot(hi, hm) + _dot(lo, hm)
            y = zc * lax.rsqrt(ms + EPS) * gain_ref[:, sl]
            partner = jnp.where(first_half, pltpu.roll(y, LANES - HEAD_DIM // 2, 1),
                                pltpu.roll(y, HEAD_DIM // 2, 1))
            out = y * cos + partner * sin
            if scale != 1.0:
                out = out * scale
            out_ref[:, sl] = out
            if t_ref is not None:
                t_ref[n_prev, sl, :] = out.T

    head_norm_rope(_dot(n, w_ref[:, 0:A_WIDTH]), qg_ref, q_ref, QK_SCALE, None)
    head_norm_rope(_dot(n, w_ref[:, A_WIDTH:2 * A_WIDTH]), kg_ref, k_ref, 1.0, kt_ref)
    v = _dot(n, w_ref[:, 2 * A_WIDTH:3 * A_WIDTH])
    v_ref[...] = v
    if transposed:
        for c in range(A_WIDTH // LANES):
            sl = slice(c * LANES, (c + 1) * LANES)
            vt_ref[n_prev, sl, :] = v[:, sl].T
    u_ref[...] = jax.nn.gelu(_dot(n, w_ref[:, 3 * A_WIDTH:3 * A_WIDTH + B_WIDTH]))
    vb = jax.nn.gelu(_dot(n, w_ref[:, 3 * A_WIDTH + B_WIDTH:IN_AB]))
    vb_ref[...] = _layernorm(vb, vbg_ref[...], vbb_ref[...])


def _const_spec(shape):
    nd = len(shape)
    return pl.BlockSpec(shape, lambda *_: (0,) * nd, pipeline_mode=pl.Buffered(1))


def _layer_spec(shape, layer):
    nd = len(shape)
    return pl.BlockSpec((None, *shape), lambda *_: (layer,) + (0,) * nd, pipeline_mode=pl.Buffered(1))


def _with_cast_duty(body, n_in, n_out, n_cast):
    def kernel(*refs):
        ins, srcs = refs[:n_in], refs[n_in:n_in + n_cast]
        outs = refs[n_in + n_cast:n_in + n_cast + n_out]
        dsts = refs[n_in + n_cast + n_out:n_in + 2 * n_cast + n_out]
        for src, dst in zip(srcs, dsts):
            dst[...] = src[...].astype(BF16)
        body(*ins, *outs, *refs[n_in + 2 * n_cast + n_out:])
    return kernel


def _cast_duty_specs(pairs, n_steps, step_of):
    in_specs, out_specs, out_shape = [], [], []
    for w, layer in pairs:
        _, rows, cols = w.shape
        per_step = rows // n_steps
        assert per_step * n_steps == rows and per_step % 16 == 0
        in_specs.append(pl.BlockSpec((None, per_step, cols), lambda *ids, layer=layer: (layer, step_of(*ids), 0)))
        out_specs.append(pl.BlockSpec((per_step, cols), lambda *ids: (step_of(*ids), 0)))
        out_shape.append(jax.ShapeDtypeStruct((rows, cols), BF16))
    return in_specs, out_specs, out_shape


def _ab_in(h, g, w_bf, j, cos_t, sin_t, qg, kg, hm, vbg, vbb, tm, seq=None, prev_t=(), cast=None):
    t = h.shape[0]
    n_pos_blocks = cos_t.shape[0] // tm
    row = lambda i: (i, 0)
    tab = lambda i: (i % n_pos_blocks, 0)
    out = jax.ShapeDtypeStruct((t, A_WIDTH), F32)
    in_specs = [
        pl.BlockSpec((tm, D_MODEL), row), _const_spec((1, D_MODEL)), _layer_spec((D_MODEL, IN_AB), j),
        pl.BlockSpec((tm, LANES), tab), pl.BlockSpec((tm, LANES), tab),
        _const_spec((1, A_WIDTH)), _const_spec((1, A_WIDTH)), _const_spec((LANES, LANES)),
        _const_spec((1, B_WIDTH)), _const_spec((1, B_WIDTH)),
    ]
    out_specs = [pl.BlockSpec((tm, A_WIDTH), row)] * 5
    out_shape = [out] * 5
    n_prev = prev_t[0].shape[0] if prev_t else 0
    if seq is not None:
        tiles_per_seq = seq // tm
        t_map = lambda i: (0, i // tiles_per_seq, 0, i % tiles_per_seq)
        if prev_t:
            in_specs += [pl.BlockSpec((n_prev, None, A_WIDTH, tm), t_map)] * 2
        out_specs += [pl.BlockSpec((n_prev + 1, None, A_WIDTH, tm), t_map)] * 2
        out_shape += [jax.ShapeDtypeStruct((n_prev + 1, t // seq, A_WIDTH, seq), F32)] * 2
    body = functools.partial(_ab_in_kernel, n_prev=n_prev, transposed=seq is not None)
    operands = [h, g, w_bf, cos_t, sin_t, qg, kg, hm, vbg, vbb, *prev_t]
    if cast:
        body = _with_cast_duty(body, len(in_specs), len(out_specs), len(cast))
        c_in, c_out, c_shape = _cast_duty_specs(cast, t // tm, lambda i: i)
        in_specs, out_specs, out_shape = in_specs + c_in, out_specs + c_out, out_shape + c_shape
        operands += [w for w, _ in cast]
    return pl.pallas_call(
        body,
        grid=(t // tm,),
        in_specs=in_specs,
        out_specs=out_specs,
        out_shape=out_shape,
        compiler_params=pltpu.CompilerParams(dimension_semantics=("arbitrary",), vmem_limit_bytes=VMEM_LIMIT),
        name="ab_in",
    )(*operands)


def _attn_prompt_kernel(q_ref, k_ref, v_ref, o_ref, acc_scr, m_scr, den_scr, *, seq):
    lane = lax.broadcasted_iota(jnp.int32, (1, LANES), 1)
    head0 = lane < HEAD_DIM
    ri = lax.broadcasted_iota(jnp.int32, (2 * Q_BLOCK, 2 * Q_BLOCK), 0) & (Q_BLOCK - 1)
    ci = lax.broadcasted_iota(jnp.int32, (2 * Q_BLOCK, 2 * Q_BLOCK), 1)
    band_two = jnp.where(ci < Q_BLOCK, ci - ri, ri + Q_BLOCK - ci) >= 0
    band_one = (lax.broadcasted_iota(jnp.int32, (2 * Q_BLOCK, Q_BLOCK), 1)
                <= lax.broadcasted_iota(jnp.int32, (2 * Q_BLOCK, Q_BLOCK), 0) & (Q_BLOCK - 1))

    def rows(start, dil):
        if dil == 1:
            return pl.ds(start if isinstance(start, int) else pl.multiple_of(start, Q_BLOCK), Q_BLOCK)
        return pl.ds(start, Q_BLOCK, stride=dil)

    def load(dil, start, with_prev):
        cur = rows(start, dil)
        q = q_ref[0, cur, :]
        q2 = jnp.concatenate([jnp.where(head0, q, 0.0), jnp.where(head0, 0.0, q)], axis=0).astype(BF16)
        k = k_ref[0, cur, :].astype(BF16)
        v = v_ref[0, cur, :].astype(BF16)
        if with_prev:
            prev = rows(start - dil * Q_BLOCK, dil)
            k = jnp.concatenate([k_ref[0, prev, :].astype(BF16), k], axis=0)
            v = jnp.concatenate([v_ref[0, prev, :].astype(BF16), v], axis=0)
        return q2, k, v

    def attend(q2, k, v, with_prev):
        n_keys = k.shape[0]
        s = lax.dot_general(q2, k, (((1,), (1,)), ((), ())), preferred_element_type=F32)
        s = jnp.where(band_two if with_prev else band_one, s, NEG)
        m = jnp.max(s, axis=-1, keepdims=True)
        p = jnp.exp2(s - m).astype(BF16)
        r = _dot(p, jnp.concatenate([v, jnp.ones((n_keys, LANES), BF16)], axis=1))
        acc, den = r[:, :LANES], r[:, LANES:]
        m = jnp.broadcast_to(m, den.shape)
        return [jnp.where(head0, x[:Q_BLOCK], x[Q_BLOCK:]) for x in (acc, m, den)]

    def blocks(branch, dil, starts, with_prev):
        loaded = [load(dil, st, with_prev) for st in starts]
        results = [attend(*ld, with_prev) for ld in loaded]
        for st, vals in zip(starts, results):
            for scr, val in zip((acc_scr, m_scr, den_scr), vals):
                scr[branch, rows(st, dil), :] = val

    def grouped_loop(total, body):
        n_full = total // ATTN_GROUP

        def loop_body(it, carry):
            body([it * ATTN_GROUP + g for g in range(ATTN_GROUP)])
            return carry

        if n_full:
            lax.fori_loop(0, n_full, loop_body, 0)
        if total % ATTN_GROUP:
            body(list(range(n_full * ATTN_GROUP, total)))

    for branch, (window, dil) in enumerate(BRANCHES):
        assert window // dil == Q_BLOCK
        sub_len = seq // dil
        n_blocks = sub_len // Q_BLOCK

        def first_body(items, branch=branch, dil=dil):
            blocks(branch, dil, items, False)

        grouped_loop(dil, first_body)
        if n_blocks > 1:
            def rest_body(items, branch=branch, dil=dil, n_blocks=n_blocks):
                starts = [it // (n_blocks - 1) + dil * Q_BLOCK * (it % (n_blocks - 1) + 1) for it in items]
                blocks(branch, dil, starts, True)

            grouped_loop(dil * (n_blocks - 1), rest_body)

    rows_per_step = 256

    def merge_body(it, carry):
        sl = pl.ds(pl.multiple_of(it * rows_per_step, rows_per_step), rows_per_step)
        ms = [m_scr[b, sl, :] for b in range(len(BRANCHES))]
        m = functools.reduce(jnp.maximum, ms)
        ws = [jnp.exp2(mb - m) for mb in ms]
        num = sum(w * acc_scr[b, sl, :] for b, w in enumerate(ws))
        den = sum(w * den_scr[b, sl, :] for b, w in enumerate(ws))
        o_ref[0, sl, :] = (num * (1.0 / den)).astype(o_ref.dtype)
        return carry

    lax.fori_loop(0, seq // rows_per_step, merge_body, 0)


def _attn(q, k, v, q_s, k_s, v_s, cache_kt, cache_vt, layer):
    b, seq, _ = q.shape
    n_slabs = A_WIDTH // LANES
    n, win_buf = cache_kt.shape[1], cache_kt.shape[-1]
    assert n == b * n_slabs
    for window, dil in BRANCHES:
        assert window <= win_buf <= PAST_LEN and dil & (dil - 1) == 0
    spec = pl.BlockSpec((1, seq, LANES), lambda i, j: (i, 0, j))
    rows = pl.BlockSpec((n, A_WIDTH), lambda i, j: (0, 0))
    cache = pl.BlockSpec((None, None, N_HEADS, HEAD_DIM, win_buf), lambda i, j: (layer, i * n_slabs + j, 0, 0, 0))
    prompt_scratch = [pltpu.VMEM((len(BRANCHES), seq, LANES), F32)] * 3
    sample_scratch = [pltpu.VMEM((N_HEADS, win_buf), F32), pltpu.VMEM((N_HEADS, LANES), F32),
                      pltpu.VMEM((N_HEADS, win_buf), F32)]

    def kernel(*refs):
        prompt_in, sample_in, (o_ref, os_ref), scratch = refs[:3], refs[3:8], refs[8:10], refs[10:]
        step = pl.program_id(0) * n_slabs + pl.program_id(1)
        _attn_sample_kernel(step, *sample_in, os_ref, *scratch[len(prompt_scratch):], win_buf=win_buf)
        _attn_prompt_kernel(*prompt_in, o_ref, *scratch[:len(prompt_scratch)], seq=seq)

    return pl.pallas_call(
        kernel,
        grid=(b, n_slabs),
        in_specs=[spec, spec, spec, rows, rows, rows, cache, cache],
        out_specs=[spec, rows],
        out_shape=[jax.ShapeDtypeStruct((b, seq, A_WIDTH), BF16), jax.ShapeDtypeStruct((n, A_WIDTH), F32)],
        scratch_shapes=prompt_scratch + sample_scratch,
        compiler_params=pltpu.CompilerParams(dimension_semantics=("arbitrary", "arbitrary"),
                                             vmem_limit_bytes=VMEM_LIMIT),
        name="attn",
    )(q, k, v, q_s, k_s, v_s, cache_kt, cache_vt)


def _attn_sample_kernel(n, q_ref, kn_ref, vn_ref, kt_ref, vt_ref, o_ref, s_scr, sn_scr, p_scr, *, win_buf):
    heads_per_slab = LANES // HEAD_DIM

    def head_columns(ref):
        r = ref[pl.ds(n, 1), :]
        slabs = [jnp.broadcast_to(r[:, c * LANES:(c + 1) * LANES], (LANES, LANES)).T for c in range(A_WIDTH // LANES)]
        return [slabs[h // heads_per_slab][(h % heads_per_slab) * HEAD_DIM:(h % heads_per_slab + 1) * HEAD_DIM, 0:1]
                for h in range(N_HEADS)]

    q_cols, kn_cols, vn_cols = head_columns(q_ref), head_columns(kn_ref), head_columns(vn_ref)
    back = win_buf - lax.broadcasted_iota(jnp.int32, (1, win_buf), 1)
    masks = [jnp.where((back & (dil - 1)) == 0, back, window + 1) <= window for window, dil in BRANCHES]
    for h in range(N_HEADS):
        q = q_cols[h]
        s_scr[h:h + 1, :] = jnp.sum(kt_ref[h] * q, axis=0, keepdims=True)
        sn_scr[h:h + 1, :] = jnp.broadcast_to(jnp.sum(q * kn_cols[h], axis=0, keepdims=True), (1, LANES))
    s = s_scr[...]
    s_new = sn_scr[:, 0:1]
    ps, pns, lses = [], [], []
    for mask in masks:
        sb = jnp.where(mask, s, NEG)
        m = jnp.maximum(jnp.max(sb, axis=1, keepdims=True), s_new)
        p = jnp.exp2(sb - m)
        pn = jnp.exp2(s_new - m)
        den = jnp.sum(p, axis=1, keepdims=True) + pn
        inv = 1.0 / den
        ps.append(p * inv)
        pns.append(pn * inv)
        lses.append(m + jnp.log2(den))
    mm = functools.reduce(jnp.maximum, lses)
    ws = [jnp.exp2(l - mm) for l in lses]
    inv = 1.0 / sum(ws)
    p_scr[...] = sum(w * p for w, p in zip(ws, ps)) * inv
    pn_all = sum(w * pn for w, pn in zip(ws, pns)) * inv
    o_cols = [jnp.sum(vt_ref[h] * p_scr[h:h + 1, :], axis=1, keepdims=True) + pn_all[h:h + 1] * vn_cols[h]
              for h in range(N_HEADS)]
    out = []
    for c in range(A_WIDTH // LANES):
        col = jnp.concatenate(o_cols[c * heads_per_slab:(c + 1) * heads_per_slab], axis=0)
        out.append(jnp.broadcast_to(col, (LANES, LANES)).T[0:1, :])
    o_ref[pl.ds(n, 1), :] = jnp.concatenate(out, axis=1)


def _mix_ffn_prompt_kernel(h_ref, att_ref, u_ref, vb_ref, wsp_ref, bsp_ref, wout_ref, gffn_ref, wup_ref, wdn_ref,
                           o_ref, gate_scr, *, tm):
    lane = lax.broadcasted_iota(jnp.int32, (1, LANES), 1)
    lo = lane < B_WIDTH // N_GROUPS
    ri = lax.broadcasted_iota(jnp.int32, (CHUNK, 2 * CHUNK), 0)
    ci = lax.broadcasted_iota(jnp.int32, (CHUNK, 2 * CHUNK), 1)
    causal = (ci & (CHUNK - 1)) <= ri
    n_pairs = B_WIDTH // LANES
    w_pairs = [jnp.where(causal, wsp_ref[p], 0.0).astype(BF16) for p in range(n_pairs)]
    for c in range(tm // CHUNK):
        rows = slice(c * CHUNK, (c + 1) * CHUNK)
        for p in range(n_pairs):
            sl = slice(p * LANES, (p + 1) * LANES)
            slab = vb_ref[rows, sl]
            rhs = jnp.concatenate([jnp.where(lo, slab, 0.0), jnp.where(lo, 0.0, slab)], axis=0).astype(BF16)
            mixed = _dot(w_pairs[p], rhs) + bsp_ref[:, sl]
            gate_scr[rows, sl] = (u_ref[rows, sl] * mixed).astype(BF16)
    mix = _dot(att_ref[...], wout_ref[0:A_WIDTH, :]) + _dot(gate_scr[...], wout_ref[A_WIDTH:, :])
    h1 = h_ref[...] + mix
    o_ref[...] = _ffn(h1, gffn_ref[...], wup_ref, wdn_ref)


def _mix_ffn_sample_kernel(h_ref, att_ref, u_ref, vb_ref, w0_ref, b0_ref, wout_ref, gffn_ref, wup_ref, wdn_ref,
                           o_ref):
    gate = (u_ref[...] * (vb_ref[...] * w0_ref[...] + b0_ref[...])).astype(BF16)
    mix = _dot(att_ref[...].astype(BF16), wout_ref[0:A_WIDTH, :]) + _dot(gate, wout_ref[A_WIDTH:, :])
    h1 = h_ref[...] + mix
    o_ref[...] = _ffn(h1, gffn_ref[...], wup_ref, wdn_ref)


def _mix_ffn(h, att, u, vb, gate_w, gate_b, wout_bf, gffn, wup_bf, wdn_bf, tm, sample, cast=None):
    t = h.shape[0]
    row = lambda i: (i, 0)
    if sample:
        body = _mix_ffn_sample_kernel
        scratch = []
    else:
        body = functools.partial(_mix_ffn_prompt_kernel, tm=tm)
        scratch = [pltpu.VMEM((tm, B_WIDTH), BF16)]
    in_specs = [
        pl.BlockSpec((tm, D_MODEL), row), pl.BlockSpec((tm, A_WIDTH), row),
        pl.BlockSpec((tm, B_WIDTH), row), pl.BlockSpec((tm, B_WIDTH), row),
        _const_spec(gate_w.shape), _const_spec(gate_b.shape),
        _const_spec((A_WIDTH + B_WIDTH, D_MODEL)), _const_spec((1, D_MODEL)),
        _const_spec((D_MODEL, D_FF)), _const_spec((D_FF, D_MODEL)),
    ]
    out_specs = [pl.BlockSpec((tm, D_MODEL), row)]
    out_shape = [jax.ShapeDtypeStruct((t, D_MODEL), F32)]
    operands = [h, att, u, vb, gate_w, gate_b, wout_bf, gffn, wup_bf, wdn_bf]
    if cast:
        body = _with_cast_duty(body, len(in_specs), len(out_specs), len(cast))
        c_in, c_out, c_shape = _cast_duty_specs(cast, t // tm, lambda i: i)
        in_specs, out_specs, out_shape = in_specs + c_in, out_specs + c_out, out_shape + c_shape
        operands += [w for w, _ in cast]
    return pl.pallas_call(
        body,
        grid=(t // tm,),
        in_specs=in_specs,
        out_specs=out_specs,
        out_shape=out_shape,
        scratch_shapes=scratch,
        compiler_params=pltpu.CompilerParams(dimension_semantics=("arbitrary",), vmem_limit_bytes=VMEM_LIMIT),
        name="mix_ffn_sample" if sample else "mix_ffn",
    )(*operands)


def _glu_in(h, gmix, win_ref):
    n = _rmsnorm(h, gmix).astype(BF16)
    a = _dot(n, win_ref[:, 0:D_MODEL])
    gate = _dot(n, win_ref[:, D_MODEL:])
    return a * jax.nn.sigmoid(gate)


def _conv_tail(h, y, cg_ref, cb_ref, wout_ref, gffn_ref, wup_ref, wdn_ref):
    z = _layernorm(y, cg_ref[...], cb_ref[...])
    z = (z * jax.nn.sigmoid(z)).astype(BF16)
    h1 = h + _dot(z, wout_ref[...])
    return _ffn(h1, gffn_ref[...], wup_ref, wdn_ref)


def _conv_ffn_prompt_kernel(h_ref, gmix_ref, win_ref, wdw_ref, bdw_ref, cg_ref, cb_ref, wout_ref, gffn_ref,
                            wup_ref, wdn_ref, o_ref, cc_ref, xbuf, xs_scr, y_scr, h1_scr, *, tm, tiles_per_seq):
    g = pl.program_id(0)
    keep = CONV_WIDTH - 1
    n_slabs = D_MODEL // LANES
    n_chunks = D_FF // FF_CHUNK

    @pl.when(g == 0)
    def _():
        h1_scr[...] = jnp.zeros(h1_scr.shape, F32)
        xbuf[0:HALO, :] = jnp.zeros((HALO, D_MODEL), F32)

    s = jnp.minimum(g, pl.num_programs(0) - 2) % tiles_per_seq
    xbuf[0:HALO, :] = jnp.where(s == 0, 0.0, xbuf[0:HALO, :])
    h = h_ref[...]
    xbuf[HALO:HALO + tm, :] = _glu_in(h, gmix_ref[...], win_ref)

    shift_rows = tm + HALO - SUBLANES
    for b in range(1, SUBLANES):
        xs_scr[b - 1] = xbuf[b:b + shift_rows, :]

    def conv_slab(col, anchor):
        cs = slice(col * LANES, (col + 1) * LANES)
        taps = [wdw_ref[w:w + 1, cs] for w in range(CONV_WIDTH)]
        bias = jnp.broadcast_to(bdw_ref[:, cs], (CONV_ROWS, LANES))
        for r0 in range(0, tm, CONV_ROWS):
            acc = jnp.where(g >= 0, bias, anchor)
            for w in range(CONV_WIDTH):
                a, b = divmod(HALO - keep + w, SUBLANES)
                rows = slice(r0 + SUBLANES * a, r0 + SUBLANES * a + CONV_ROWS)
                x = xbuf[rows, cs] if b == 0 else xs_scr[b - 1, rows, cs]
                acc = acc + x * jnp.where(g >= 0, taps[w], anchor[0:1])
            y_scr[r0:r0 + CONV_ROWS, cs] = acc
            anchor = acc
        return anchor

    h1 = h1_scr[(g + 1) % 2]
    n = _rmsnorm(h1, gffn_ref[...]).astype(BF16)
    acc = h1
    slabs_per_chunk = n_slabs // n_chunks
    for c in range(n_chunks):
        a = jnp.maximum(_dot(n, wup_ref[:, c * FF_CHUNK:(c + 1) * FF_CHUNK]), 0.0)
        anchor = a[0:CONV_ROWS, 0:LANES]
        for i in range(slabs_per_chunk):
            anchor = conv_slab(c * slabs_per_chunk + i, anchor)
        acc = acc + _dot((a * a).astype(BF16), wdn_ref[c * FF_CHUNK:(c + 1) * FF_CHUNK, :])
    o_ref[...] = acc

    cc_ref[0] = xbuf[HALO + tm - keep:HALO + tm, :]
    xbuf[0:HALO, :] = xbuf[tm:tm + HALO, :]
    z = _layernorm(y_scr[...], cg_ref[...], cb_ref[...])
    z = (z * jax.nn.sigmoid(z)).astype(BF16)
    h1_scr[g % 2] = h + _dot(z, wout_ref[...])


def _conv_ffn_sample_kernel(h_ref, gmix_ref, win_ref, wdw_ref, bdw_ref, cg_ref, cb_ref, wout_ref, gffn_ref,
                            wup_ref, wdn_ref, st_ref, o_ref, cc_ref):
    keep = CONV_WIDTH - 1
    h = h_ref[...]
    x = _glu_in(h, gmix_ref[...], win_ref)
    y = x * wdw_ref[keep:keep + 1, :] + bdw_ref[...]
    for w in range(keep):
        y = y + st_ref[w] * wdw_ref[w:w + 1, :]
    for w in range(keep - 1):
        cc_ref[w] = st_ref[w + 1]
    cc_ref[keep - 1] = x
    o_ref[...] = _conv_tail(h, y, cg_ref, cb_ref, wout_ref, gffn_ref, wup_ref, wdn_ref)


def _conv_weight_specs():
    return [
        _const_spec((1, D_MODEL)), _const_spec((D_MODEL, 2 * D_MODEL)), _const_spec((CONV_WIDTH, D_MODEL)),
        _const_spec((1, D_MODEL)), _const_spec((1, D_MODEL)), _const_spec((1, D_MODEL)),
        _const_spec((D_MODEL, D_MODEL)), _const_spec((1, D_MODEL)),
        _const_spec((D_MODEL, D_FF)), _const_spec((D_FF, D_MODEL)),
    ]


def _conv_ffn_prompt(h, weights, batch, seq, tm, cast=None):
    keep = CONV_WIDTH - 1
    n_s = seq // tm
    n_tiles = batch * n_s
    conv_tile = lambda g: (jnp.minimum(g, n_tiles - 1), 0)
    ffn_tile = lambda g: (jnp.maximum(g - 1, 0), 0)
    body = functools.partial(_conv_ffn_prompt_kernel, tm=tm, tiles_per_seq=n_s)
    in_specs = [pl.BlockSpec((tm, D_MODEL), conv_tile)] + _conv_weight_specs()
    out_specs = [pl.BlockSpec((tm, D_MODEL), ffn_tile),
                 pl.BlockSpec((1, keep, D_MODEL), lambda g: (jnp.minimum(g, n_tiles - 1) // n_s, 0, 0))]
    out_shape = [jax.ShapeDtypeStruct((batch * seq, D_MODEL), F32),
                 jax.ShapeDtypeStruct((batch, keep, D_MODEL), F32)]
    operands = [h, *weights]
    if cast:
        body = _with_cast_duty(body, len(in_specs), len(out_specs), len(cast))
        c_in, c_out, c_shape = _cast_duty_specs(cast, n_tiles, lambda g: jnp.minimum(g, n_tiles - 1))
        in_specs, out_specs, out_shape = in_specs + c_in, out_specs + c_out, out_shape + c_shape
        operands += [w for w, _ in cast]
    return pl.pallas_call(
        body,
        grid=(n_tiles + 1,),
        in_specs=in_specs,
        out_specs=out_specs,
        out_shape=out_shape,
        scratch_shapes=[pltpu.VMEM((HALO + tm, D_MODEL), F32),
                        pltpu.VMEM((SUBLANES - 1, HALO + tm - SUBLANES, D_MODEL), F32),
                        pltpu.VMEM((tm, D_MODEL), F32),
                        pltpu.VMEM((2, tm, D_MODEL), F32)],
        compiler_params=pltpu.CompilerParams(dimension_semantics=("arbitrary",), vmem_limit_bytes=VMEM_LIMIT),
        name="conv_ffn",
    )(*operands)


def _conv_ffn_sample(h, weights, j, state_t):
    n = h.shape[0]
    keep = CONV_WIDTH - 1
    return pl.pallas_call(
        _conv_ffn_sample_kernel,
        grid=(1,),
        in_specs=[_const_spec((n, D_MODEL))] + _conv_weight_specs()
        + [pl.BlockSpec((None, keep, n, D_MODEL), lambda i: (j, 0, 0, 0))],
        out_specs=[_const_spec((n, D_MODEL)), _const_spec((keep, n, D_MODEL))],
        out_shape=[jax.ShapeDtypeStruct((n, D_MODEL), F32), jax.ShapeDtypeStruct((keep, n, D_MODEL), F32)],
        compiler_params=pltpu.CompilerParams(dimension_semantics=("arbitrary",), vmem_limit_bytes=VMEM_LIMIT),
        name="conv_ffn_sample",
    )(h, *weights, state_t)


def _rope_tables(pos):
    inv = ROPE_THETA ** (-jnp.arange(0, HEAD_DIM, 2, dtype=F32) / HEAD_DIM)
    ang = pos.astype(F32)[:, None] * inv[None, :]
    cos, sin = jnp.cos(ang), jnp.sin(ang)
    reps = LANES // HEAD_DIM
    return jnp.tile(jnp.concatenate([cos, cos], axis=-1), (1, reps)), jnp.tile(jnp.concatenate([-sin, sin], axis=-1), (1, reps))


def kernel(x_prompt, x_sample, cache_a_k, cache_a_v, state_c_conv, norm_mix_g, norm_ffn_g, w_ffn_up, w_ffn_down, w_in_ab, q_norm_g, k_norm_g, vb_norm_g, vb_norm_b, w_spatial, b_spatial, w_out_ab, w_c_in, w_c_dw, b_c_dw, c_norm_g, c_norm_b, w_c_out):
    batch, seq, _ = x_prompt.shape
    n_dec, dec_seq, _ = x_sample.shape
    depth = norm_mix_g.shape[0]
    n_ab, _, win_buf = cache_a_k.shape[:3]
    past_len = PAST_LEN
    assert dec_seq == 1 and seq % (BRANCHES[-1][1] * Q_BLOCK) == 0 and seq % CHUNK == 0
    tm = 256

    hp = x_prompt.reshape(batch * seq, D_MODEL)
    hs = x_sample.reshape(n_dec * dec_seq, D_MODEL)
    cos_p, sin_p = _rope_tables(jnp.arange(seq, dtype=jnp.int32))
    cos_s, sin_s = _rope_tables(jnp.full((n_dec,), past_len, dtype=jnp.int32))
    head_id = jnp.arange(LANES) // HEAD_DIM
    head_mean = jnp.where(head_id[:, None] == head_id[None, :], 1.0 / HEAD_DIM, 0.0).astype(BF16)
    cache_kt = cache_a_k.transpose(0, 1, 3, 4, 2)
    cache_vt = cache_a_v.transpose(0, 1, 3, 4, 2)
    state_t = state_c_conv.transpose(0, 2, 1, 3)
    group_dim = B_WIDTH // N_GROUPS
    row2 = lambda a: a.reshape(1, -1)

    w_in = w_in_ab.astype(BF16)
    ffn_f32 = (w_ffn_up, w_ffn_down)
    ffn_bf = {}
    next_ffn = lambda layer: [(w, layer + 1) for w in ffn_f32] if layer + 1 < depth else []

    ak_s, av_s, bv_p, bv_s, cc_p, cc_s = [], [], [], [], [], []
    kv_t = ()
    for layer in range(depth):
        j = layer // 2
        gffn = row2(norm_ffn_g[layer])
        gmix = row2(norm_mix_g[layer])
        if layer % 2 == 0:
            qg = row2(jnp.tile(q_norm_g[j], N_HEADS))
            kg = row2(jnp.tile(k_norm_g[j], N_HEADS))
            vbg, vbb = row2(vb_norm_g[j]), row2(vb_norm_b[j])
            ab_cast = [(w_out_ab, j)] + ([] if layer in ffn_bf else [(w, layer) for w in ffn_f32])
            qp, kp, vp, up, vbp, *rest = _ab_in(hp, gmix, w_in, j, cos_p, sin_p, qg, kg, head_mean, vbg, vbb, 2 * tm,
                                                seq=seq, prev_t=tuple(kv_t), cast=ab_cast)
            kv_t, wout = rest[:2], rest[2]
            if layer not in ffn_bf:
                ffn_bf[layer] = tuple(rest[3:])
            wup, wdn = ffn_bf[layer]
            qs, ks, vs, us, vbs = _ab_in(hs, gmix, w_in, j, cos_s, sin_s, qg, kg, head_mean, vbg, vbb, n_dec)
            shp = (batch, seq, A_WIDTH)
            att_p, att_s = _attn(qp.reshape(shp), kp.reshape(shp), vp.reshape(shp), qs, ks, vs, cache_kt, cache_vt, j)
            att_p = att_p.reshape(batch * seq, A_WIDTH)
            w_pairs = w_spatial[j].reshape(N_GROUPS // 2, 2, CHUNK, CHUNK).transpose(0, 2, 1, 3).reshape(N_GROUPS // 2, CHUNK, 2 * CHUNK)
            b_rows = jnp.repeat(b_spatial[j].T, group_dim, axis=1)
            w0 = row2(jnp.repeat(w_spatial[j][:, 0, 0], group_dim))
            b0 = row2(jnp.repeat(b_spatial[j][:, 0], group_dim))
            conv_cast = [(w_c_in, j), (w_c_out, j)] if layer + 1 < depth else []
            hp, *cast_out = _mix_ffn(hp, att_p, up, vbp, w_pairs, b_rows, wout, gffn, wup, wdn, tm, False,
                                     cast=conv_cast + next_ffn(layer))
            if conv_cast:
                (wc_in, wc_out), cast_out = cast_out[:2], cast_out[2:]
            hs, = _mix_ffn(hs, att_s, us, vbs, w0, b0, wout, gffn, wup, wdn, n_dec, True)
            ak_s.append(ks.reshape(n_dec, dec_seq, N_HEADS, HEAD_DIM))
            av_s.append(vs.reshape(n_dec, dec_seq, N_HEADS, HEAD_DIM))
            last_chunk_start = ((seq - 1) // CHUNK) * CHUNK
            bv_p.append(vbp.reshape(batch, seq, B_WIDTH)[:, last_chunk_start:])
            bv_s.append(vbs.reshape(n_dec, dec_seq, B_WIDTH))
        else:
            wup, wdn = ffn_bf[layer]
            weights = (gmix, wc_in, w_c_dw[j], row2(b_c_dw[j]), row2(c_norm_g[j]),
                       row2(c_norm_b[j]), wc_out, gffn, wup, wdn)
            hp, new_cp, *cast_out = _conv_ffn_prompt(hp, weights, batch, seq, tm, cast=next_ffn(layer))
            hs, new_cs = _conv_ffn_sample(hs, weights, j, state_t)
            cc_p.append(new_cp)
            cc_s.append(new_cs)
        if cast_out:
            ffn_bf[layer + 1] = tuple(cast_out)

    prompt_buf = min(BRANCHES[-1][0], seq)
    ak_p, av_p = [a.reshape(n_ab, batch, N_HEADS, HEAD_DIM, seq).transpose(0, 1, 4, 2, 3)[:, :, seq - prompt_buf:]
                  for a in kv_t]
    return (hp.reshape(batch, seq, D_MODEL), hs.reshape(n_dec, dec_seq, D_MODEL),
            ak_p, av_p, jnp.stack(ak_s), jnp.stack(av_s),
            jnp.stack(bv_p), jnp.stack(bv_s), jnp.stack(cc_p), jnp.stack(cc_s).transpose(0, 2, 1, 3))
```
